```python
import math
import jax
import jax.numpy as jnp
from jax import lax
import numpy as np

D_MODEL = 2048
BATCH = 2
SEQ = 4096
DEPTH = 2
DEC_BATCH = 128
DEC_SEQ = 1
PAST_LEN = 16384
PAGE_SIZE = 128

N_EVEN = (DEPTH + 1) // 2
N_ODD = DEPTH // 2

MLA_HEADS = 8
QK_NOPE = 128
QK_ROPE = 64
V_HEAD = 128
KV_RANK = 512
ROPE_THETA = 10000.0
MLA_Q_DIM = MLA_HEADS * (QK_NOPE + QK_ROPE)
MLA_KVA_DIM = KV_RANK + QK_ROPE
MLA_OUT = MLA_HEADS * V_HEAD
ATTN_SCALE = (QK_NOPE + QK_ROPE) ** -0.5
QBLOCK = 128

LRU_WIDTH = D_MODEL // 2
LRU_HEADS = 8
LRU_HEAD_DIM = LRU_WIDTH // LRU_HEADS
CONV_W = 4
LRU_C = 8.0

AB_IN = MLA_Q_DIM + MLA_KVA_DIM + 2 * LRU_WIDTH
AB_MIX = MLA_OUT + LRU_WIDTH

GMLP_FFN = 2 * D_MODEL
GMLP_HALF = GMLP_FFN // 2
CHUNK = 128
GMLP_GROUPS = 8
GMLP_GROUP_DIM = GMLP_HALF // GMLP_GROUPS

D_FF = 4 * D_MODEL

EPS = 1e-6
NEG_INF = -1e30

kernel_name = 'hybrid_rglru_mla_gmlp_decoder_step'


def rms_norm(x, g):
    xf = x.astype(jnp.float32)
    y = xf * lax.rsqrt(jnp.mean(xf * xf, axis=-1, keepdims=True) + EPS)
    return (y * g.astype(jnp.float32)).astype(x.dtype)


def layer_norm(x, g, b):
    xf = x.astype(jnp.float32)
    mu = jnp.mean(xf, axis=-1, keepdims=True)
    var = jnp.mean(jnp.square(xf - mu), axis=-1, keepdims=True)
    y = (xf - mu) * lax.rsqrt(var + EPS) * g.astype(jnp.float32) + b.astype(jnp.float32)
    return y.astype(x.dtype)


def rope(x, pos):
    half = QK_ROPE // 2
    inv = ROPE_THETA ** (-jnp.arange(half, dtype=jnp.float32) / half)
    ang = pos.astype(jnp.float32)[:, None] * inv[None, :]
    cos = jnp.cos(ang)[None, :, None, :]
    sin = jnp.sin(ang)[None, :, None, :]
    xf = x.astype(jnp.float32)
    x1, x2 = xf[..., :half], xf[..., half:]
    return jnp.concatenate([x1 * cos - x2 * sin, x1 * sin + x2 * cos], axis=-1).astype(x.dtype)


def causal_conv(x, buf, w, b):
    s = x.shape[1]
    xp = jnp.concatenate([buf.astype(x.dtype), x], axis=1)
    y = b
    for k in range(CONV_W):
        y = y + xp[:, k:k + s] * w[k]
    return y.astype(x.dtype), xp[:, s:]


def rglru(xc, h0, w_a, b_a, w_x, b_x, lam):
    bsz, s, _ = xc.shape
    xh = xc.reshape(bsz, s, LRU_HEADS, LRU_HEAD_DIM)
    gate_r = jnp.einsum('bshi,hij->bshj', xh, w_a).reshape(bsz, s, LRU_WIDTH) + b_a
    gate_i = jnp.einsum('bshi,hij->bshj', xh, w_x).reshape(bsz, s, LRU_WIDTH) + b_x
    r = jax.nn.sigmoid(gate_r.astype(jnp.float32))
    i = jax.nn.sigmoid(gate_i.astype(jnp.float32))
    log_a = -LRU_C * r * jax.nn.softplus(-lam.astype(jnp.float32))
    a = jnp.exp(log_a)
    u = jnp.sqrt(-jnp.expm1(2.0 * log_a)) * i * xc.astype(jnp.float32)

    def step(h, inp):
        a_t, u_t = inp
        h = a_t * h + u_t
        return h, h

    h_last, hs = lax.scan(step, h0.astype(jnp.float32), (jnp.swapaxes(a, 0, 1), jnp.swapaxes(u, 0, 1)))
    return jnp.swapaxes(hs, 0, 1).astype(xc.dtype), h_last


def attend_prompt(q_lat, q_pe, c_kv, k_pe):
    bsz, s, h, c = q_lat.shape
    nq = s // QBLOCK
    ckv = c_kv.astype(jnp.float32)
    kpe = k_pe.astype(jnp.float32)
    k_pos = jnp.arange(s)
    qb = jnp.swapaxes(q_lat.astype(jnp.float32).reshape(bsz, nq, QBLOCK, h, c), 0, 1)
    pb = jnp.swapaxes(q_pe.astype(jnp.float32).reshape(bsz, nq, QBLOCK, h, QK_ROPE), 0, 1)
    starts = jnp.arange(nq) * QBLOCK

    def block(args):
        ql, qp, s0 = args
        sc = (jnp.einsum('bqhc,bkc->bhqk', ql, ckv) + jnp.einsum('bqhr,bkr->bhqk', qp, kpe)) * ATTN_SCALE
        q_pos = s0 + jnp.arange(QBLOCK)
        sc = jnp.where(k_pos[None, :] <= q_pos[:, None], sc, NEG_INF)
        p = jax.nn.softmax(sc, axis=-1)
        return jnp.einsum('bhqk,bkc->bqhc', p, ckv)

    o = lax.map(block, (qb, pb, starts))
    return jnp.swapaxes(o, 0, 1).reshape(bsz, s, h, c)


def make_attend_paged(cache_latent, cache_kpe, page_table, e):
    def attend(q_lat, q_pe, c_kv, k_pe):
        bsz, s, h, c = q_lat.shape
        ql = q_lat.astype(jnp.float32)
        qp = q_pe.astype(jnp.float32)

        def scores(lat, kpe):
            return (jnp.einsum('bqhc,bkc->bhqk', ql, lat.astype(jnp.float32))
                    + jnp.einsum('bqhr,bkr->bhqk', qp, kpe.astype(jnp.float32))) * ATTN_SCALE

        def update(carry, sc, vals):
            m, l, acc = carry
            m_new = jnp.maximum(m, jnp.max(sc, axis=-1))
            alpha = jnp.exp(m - m_new)
            p = jnp.exp(sc - m_new[..., None])
            acc = acc * alpha[..., None] + jnp.einsum('bhqk,bkc->bhqc', p, vals.astype(jnp.float32))
            return (m_new, l * alpha + jnp.sum(p, axis=-1), acc)

        init = (jnp.full((bsz, h, s), NEG_INF, jnp.float32),
                jnp.zeros((bsz, h, s), jnp.float32),
                jnp.zeros((bsz, h, s, c), jnp.float32))

        def page_step(carry, pages):
            lat = cache_latent[e, pages]
            kpe = cache_kpe[e, pages]
            return update(carry, scores(lat, kpe), lat), None

        carry, _ = lax.scan(page_step, init, page_table.T)
        causal = jnp.tril(jnp.ones((s, s), dtype=bool))
        sc_new = jnp.where(causal, scores(c_kv, k_pe), NEG_INF)
        _, l, acc = update(carry, sc_new, c_kv)
        return jnp.transpose(acc / l[..., None], (0, 2, 1, 3))

    return attend


def ab_mixer(hn, pos, attend, conv_buf, h0, w_in, g_kv, w_uk, w_uv, conv_w, conv_b,
             w_a, b_a, w_x, b_x, lam, w_out):
    bsz, s, _ = hn.shape
    proj = hn @ w_in
    q, kva, xb, gb = jnp.split(proj, [MLA_Q_DIM, MLA_Q_DIM + MLA_KVA_DIM,
                                      MLA_Q_DIM + MLA_KVA_DIM + LRU_WIDTH], axis=-1)
    q = q.reshape(bsz, s, MLA_HEADS, QK_NOPE + QK_ROPE)
    q_nope = q[..., :QK_NOPE]
    q_pe = rope(q[..., QK_NOPE:], pos)
    c_kv = rms_norm(kva[..., :KV_RANK], g_kv)
    k_pe = rope(kva[..., None, KV_RANK:], pos)[:, :, 0]
    q_lat = jnp.einsum('bshd,chd->bshc', q_nope, w_uk)
    o_lat = attend(q_lat, q_pe, c_kv, k_pe)
    o_mla = jnp.einsum('bshc,chd->bshd', o_lat.astype(hn.dtype), w_uv).reshape(bsz, s, MLA_OUT)
    xc, conv_new = causal_conv(xb, conv_buf, conv_w, conv_b)
    y_lru, h_new = rglru(xc, h0, w_a, b_a, w_x, b_x, lam)
    o_lru = y_lru * jax.nn.gelu(gb)
    out = jnp.concatenate([o_mla, o_lru], axis=-1) @ w_out
    return out, c_kv, k_pe, h_new, conv_new


def c_mixer(hn, w_in, ln_g, ln_b, w_s, b_s, w_out):
    bsz, s, _ = hn.shape
    z = jax.nn.gelu(hn @ w_in)
    u, v = z[..., :GMLP_HALF], z[..., GMLP_HALF:]
    v = layer_norm(v, ln_g, ln_b)
    n = -(-s // CHUNK)
    vp = jnp.pad(v, ((0, 0), (0, n * CHUNK - s), (0, 0))).reshape(bsz, n, CHUNK, GMLP_GROUPS, GMLP_GROUP_DIM)
    w_causal = w_s * jnp.tril(jnp.ones((CHUNK, CHUNK), w_s.dtype))
    sg = jnp.einsum('gts,bnsgc->bntgc', w_causal, vp) + b_s.T[None, None, :, :, None]
    sg = sg.reshape(bsz, n * CHUNK, GMLP_HALF)[:, :s]
    return (u * sg).astype(hn.dtype) @ w_out, v


def run_trunk(x, pos, attends, conv_bufs, h0s, p):
    h = x
    lat_rows, kpe_rows, lru_hs, convs, v_rows = [], [], [], [], []
    for l in range(DEPTH):
        j = l // 2
        hn = rms_norm(h, p['norm_mix'][l])
        if l % 2 == 0:
            mix, c_kv, k_pe, h_new, conv_new = ab_mixer(
                hn, pos, attends[j], conv_bufs[j], h0s[j], p['ab_w_in'][j], p['ab_g_kv'][j],
                p['ab_w_uk'][j], p['ab_w_uv'][j], p['ab_conv_w'][j], p['ab_conv_b'][j],
                p['ab_w_a'][j], p['ab_b_a'][j], p['ab_w_x'][j], p['ab_b_x'][j],
                p['ab_lambda'][j], p['ab_w_out'][j])
            lat_rows.append(c_kv)
            kpe_rows.append(k_pe)
            lru_hs.append(h_new.astype(x.dtype))
            convs.append(conv_new)
        else:
            mix, v = c_mixer(hn, p['c_w_in'][j], p['c_ln_g'][j], p['c_ln_b'][j],
                             p['c_w_s'][j], p['c_b_s'][j], p['c_w_out'][j])
            v_rows.append(v)
        h = h + mix
        hn = rms_norm(h, p['norm_ffn'][l])
        h = h + jnp.square(jax.nn.relu(hn @ p['ffn_w1'][l])) @ p['ffn_w2'][l]
    y = rms_norm(h, p['norm_final'])
    return y, jnp.stack(lat_rows), jnp.stack(kpe_rows), jnp.stack(lru_hs), jnp.stack(convs), jnp.stack(v_rows)


def setup_inputs(seed: int = 0) -> dict:
    key = jax.random.key(seed)
    ks = iter(jax.random.split(key, 48))

    def nrm(shape, scale):
        return jax.random.normal(next(ks), shape, jnp.float32) * scale

    n_pages = PAST_LEN // PAGE_SIZE
    n_pool = (DEC_BATCH * n_pages * 5) // 4
    perm = jax.random.permutation(next(ks), n_pool)
    page_table = perm[:DEC_BATCH * n_pages].reshape(DEC_BATCH, n_pages).astype(jnp.int32)
    a0 = jax.random.uniform(next(ks), (N_EVEN, LRU_WIDTH), jnp.float32, 0.9, 0.999)
    lam = jnp.log(a0) - jnp.log1p(-a0)
    return {
        'x_prompt': nrm((BATCH, SEQ, D_MODEL), 1.0),
        'x_sample': nrm((DEC_BATCH, DEC_SEQ, D_MODEL), 1.0),
        'cache_latent': nrm((N_EVEN, n_pool, PAGE_SIZE, KV_RANK), 1.0),
        'cache_kpe': nrm((N_EVEN, n_pool, PAGE_SIZE, QK_ROPE), 1.0),
        'state_lru_h': nrm((N_EVEN, DEC_BATCH, LRU_WIDTH), 0.5),
        'state_conv': nrm((N_EVEN, DEC_BATCH, CONV_W - 1, LRU_WIDTH), 1.0),
        'page_table': page_table,
        'norm_mix': 1.0 + nrm((DEPTH, D_MODEL), 0.02),
        'norm_ffn': 1.0 + nrm((DEPTH, D_MODEL), 0.02),
        'norm_final': 1.0 + nrm((D_MODEL,), 0.02),
        'ab_w_in': nrm((N_EVEN, D_MODEL, AB_IN), D_MODEL ** -0.5),
        'ab_g_kv': 1.0 + nrm((N_EVEN, KV_RANK), 0.02),
        'ab_w_uk': nrm((N_EVEN, KV_RANK, MLA_HEADS, QK_NOPE), KV_RANK ** -0.5),
        'ab_w_uv': nrm((N_EVEN, KV_RANK, MLA_HEADS, V_HEAD), KV_RANK ** -0.5),
        'ab_conv_w': nrm((N_EVEN, CONV_W, LRU_WIDTH), CONV_W ** -0.5),
        'ab_conv_b': nrm((N_EVEN, LRU_WIDTH), 0.02),
        'ab_w_a': nrm((N_EVEN, LRU_HEADS, LRU_HEAD_DIM, LRU_HEAD_DIM), LRU_HEAD_DIM ** -0.5),
        'ab_b_a': nrm((N_EVEN, LRU_WIDTH), 0.02),
        'ab_w_x': nrm((N_EVEN, LRU_HEADS, LRU_HEAD_DIM, LRU_HEAD_DIM), LRU_HEAD_DIM ** -0.5),
        'ab_b_x': nrm((N_EVEN, LRU_WIDTH), 0.02),
        'ab_lambda': lam,
        'ab_w_out': nrm((N_EVEN, AB_MIX, D_MODEL), AB_MIX ** -0.5),
        'c_w_in': nrm((N_ODD, D_MODEL, GMLP_FFN), D_MODEL ** -0.5),
        'c_ln_g': 1.0 + nrm((N_ODD, GMLP_HALF), 0.02),
        'c_ln_b': nrm((N_ODD, GMLP_HALF), 0.02),
        'c_w_s': nrm((N_ODD, GMLP_GROUPS, CHUNK, CHUNK), CHUNK ** -0.5),
        'c_b_s': 1.0 + nrm((N_ODD, GMLP_GROUPS, CHUNK), 0.02),
        'c_w_out': nrm((N_ODD, GMLP_HALF, D_MODEL), GMLP_HALF ** -0.5),
        'ffn_w1': nrm((DEPTH, D_MODEL, D_FF), D_MODEL ** -0.5),
        'ffn_w2': nrm((DEPTH, D_FF, D_MODEL), D_FF ** -0.5),
    }


def reference(x_prompt, x_sample, cache_latent, cache_kpe, state_lru_h, state_conv, page_table,
              norm_mix, norm_ffn, norm_final, ab_w_in, ab_g_kv, ab_w_uk, ab_w_uv, ab_conv_w, ab_conv_b,
              ab_w_a, ab_b_a, ab_w_x, ab_b_x, ab_lambda, ab_w_out, c_w_in, c_ln_g, c_ln_b, c_w_s, c_b_s,
              c_w_out, ffn_w1, ffn_w2):
    p = dict(norm_mix=norm_mix, norm_ffn=norm_ffn, norm_final=norm_final, ab_w_in=ab_w_in,
             ab_g_kv=ab_g_kv, ab_w_uk=ab_w_uk, ab_w_uv=ab_w_uv, ab_conv_w=ab_conv_w, ab_conv_b=ab_conv_b,
             ab_w_a=ab_w_a, ab_b_a=ab_b_a, ab_w_x=ab_w_x, ab_b_x=ab_b_x, ab_lambda=ab_lambda,
             ab_w_out=ab_w_out, c_w_in=c_w_in, c_ln_g=c_ln_g, c_ln_b=c_ln_b, c_w_s=c_w_s, c_b_s=c_b_s,
             c_w_out=c_w_out, ffn_w1=ffn_w1, ffn_w2=ffn_w2)
    b, s = x_prompt.shape[0], x_prompt.shape[1]
    pos_p = jnp.arange(s, dtype=jnp.int32)
    conv0_p = jnp.zeros((N_EVEN, b, CONV_W - 1, LRU_WIDTH), x_prompt.dtype)
    h0_p = jnp.zeros((N_EVEN, b, LRU_WIDTH), jnp.float32)
    attends_p = [attend_prompt for _ in range(N_EVEN)]
    y_prompt, lat_p, kpe_p, lru_h_p, conv_p, _ = run_trunk(x_prompt, pos_p, attends_p, conv0_p, h0_p, p)
    ds = x_sample.shape[1]
    pos_s = PAST_LEN + jnp.arange(ds, dtype=jnp.int32)
    attends_s = [make_attend_paged(cache_latent, cache_kpe, page_table, j) for j in range(N_EVEN)]
    y_sample, lat_s, kpe_s, lru_h_s, conv_s, v_s = run_trunk(x_sample, pos_s, attends_s, state_conv, state_lru_h, p)
    return (y_prompt, y_sample, lat_p, kpe_p, lru_h_p, conv_p, lat_s, kpe_s, lru_h_s, conv_s, v_s)
```

```python
import functools
from typing import NamedTuple

import numpy as np
import jax
import jax.numpy as jnp
from jax import lax
from jax.experimental import pallas as pl
from jax.experimental.pallas import tpu as pltpu

D_MODEL = 2048
MLA_HEADS = 8
QK_NOPE = 128
QK_ROPE = 64
V_HEAD = 128
KV_RANK = 512
ROPE_THETA = 10000.0
ATTN_SCALE = (QK_NOPE + QK_ROPE) ** -0.5
LRU_WIDTH = D_MODEL // 2
LRU_HEADS = 8
LRU_HEAD_DIM = LRU_WIDTH // LRU_HEADS
CONV_W = 4
LRU_C = 8.0
GMLP_HALF = D_MODEL
CHUNK = 128
GMLP_GROUPS = 8
GMLP_GROUP_DIM = GMLP_HALF // GMLP_GROUPS
D_FF = 4 * D_MODEL
EPS = 1e-6
NEG_INF = -1e30
PAGE_SIZE = 128

LANES = 128
V7X_VMEM_BYTES = 64 * 1024 * 1024
VMEM_LIMIT = (V7X_VMEM_BYTES * 7) // 8

QK_WIDTH = KV_RANK + LANES

COL_QN = 0
COL_XB = 1024
COL_GB = 2048
COL_QP = 3072
COL_CKV = 3584
COL_KPE = 4096
PROJ_WIDTH = 4224

BF16 = jnp.bfloat16
F32 = jnp.float32


def _cparams(*sem):
    return pltpu.CompilerParams(dimension_semantics=sem, vmem_limit_bytes=VMEM_LIMIT)


def _lane_repeat(x, n):
    return jnp.concatenate([x] * n, axis=1)


def _rms(x, g):
    return x * lax.rsqrt(jnp.mean(x * x, axis=-1, keepdims=True) + EPS) * g


def _norm_matmul_kernel(x_ref, g_ref, w_ref, o_ref, xn_ref, *, act):
    @pl.when(pl.program_id(1) == 0)
    def _():
        xn_ref[...] = _rms(x_ref[...], g_ref[...]).astype(BF16)

    y = jnp.dot(xn_ref[...], w_ref[...], preferred_element_type=F32)
    if act == "gelu":
        y = jax.nn.gelu(y)
    o_ref[...] = y


def _norm_matmul(x, g, w, *, tm, tn, act=None):
    t, k = x.shape
    n = w.shape[1]
    return pl.pallas_call(
        functools.partial(_norm_matmul_kernel, act=act),
        grid=(t // tm, n // tn),
        in_specs=[
            pl.BlockSpec((tm, k), lambda i, j: (i, 0)),
            pl.BlockSpec((1, k), lambda i, j: (0, 0)),
            pl.BlockSpec((k, tn), lambda i, j: (0, j)),
        ],
        out_specs=pl.BlockSpec((tm, tn), lambda i, j: (i, j)),
        out_shape=jax.ShapeDtypeStruct((t, n), F32),
        scratch_shapes=[pltpu.VMEM((tm, k), BF16)],
        compiler_params=_cparams("parallel", "arbitrary"),
        name="norm_matmul",
    )(x, g.reshape(1, k), w)


def _pair_matmul_res_kernel(a_ref, b_ref, wa_ref, wb_ref, r_ref, o_ref):
    y = jnp.dot(a_ref[...], wa_ref[...], preferred_element_type=F32)
    y = y + jnp.dot(b_ref[...], wb_ref[...], preferred_element_type=F32)
    o_ref[...] = r_ref[...] + y


def _pair_matmul_res(a, b, w, res, *, tm, tn):
    t, ka = a.shape
    kb = b.shape[1]
    assert ka == kb and w.shape[0] == ka + kb
    n = w.shape[1]
    return pl.pallas_call(
        _pair_matmul_res_kernel,
        grid=(t // tm, n // tn),
        in_specs=[
            pl.BlockSpec((tm, ka), lambda i, j: (i, 0)),
            pl.BlockSpec((tm, kb), lambda i, j: (i, 0)),
            pl.BlockSpec((ka, tn), lambda i, j: (0, j)),
            pl.BlockSpec((kb, tn), lambda i, j: (1, j)),
            pl.BlockSpec((tm, tn), lambda i, j: (i, j)),
        ],
        out_specs=pl.BlockSpec((tm, tn), lambda i, j: (i, j)),
        out_shape=jax.ShapeDtypeStruct((t, n), F32),
        compiler_params=_cparams("parallel", "arbitrary"),
        name="pair_matmul_res",
    )(a, b, w, w, res)


def _ffn_kernel(x_ref, g_ref, w1_ref, w2_ref, gf_ref, o_ref, xn_ref, *, final_norm, col_chunk):
    j = pl.program_id(1)

    @pl.when(j == 0)
    def _():
        x = x_ref[...]
        xn_ref[...] = _rms(x, g_ref[...]).astype(BF16)
        o_ref[...] = x

    a = jnp.dot(xn_ref[...], w1_ref[...], preferred_element_type=F32)
    a = jnp.square(jnp.maximum(a, 0.0)).astype(BF16)
    for c in range(o_ref.shape[1] // col_chunk):
        sl = slice(c * col_chunk, (c + 1) * col_chunk)
        o_ref[:, sl] += jnp.dot(a, w2_ref[:, sl], preferred_element_type=F32)

    if final_norm:
        @pl.when(j == pl.num_programs(1) - 1)
        def _():
            o_ref[...] = _rms(o_ref[...], gf_ref[...])


def _ffn(x, g, w1, w2, g_final, *, tm, tf, final_norm):
    t, d = x.shape
    f = w1.shape[1]
    return pl.pallas_call(
        functools.partial(_ffn_kernel, final_norm=final_norm, col_chunk=512),
        grid=(t // tm, f // tf),
        in_specs=[
            pl.BlockSpec((tm, d), lambda i, j: (i, 0)),
            pl.BlockSpec((1, d), lambda i, j: (0, 0)),
            pl.BlockSpec((d, tf), lambda i, j: (0, j)),
            pl.BlockSpec((tf, d), lambda i, j: (j, 0)),
            pl.BlockSpec((1, d), lambda i, j: (0, 0)),
        ],
        out_specs=pl.BlockSpec((tm, d), lambda i, j: (i, 0)),
        out_shape=jax.ShapeDtypeStruct((t, d), F32),
        scratch_shapes=[pltpu.VMEM((tm, d), BF16)],
        compiler_params=_cparams("parallel", "arbitrary"),
        name="ffn",
    )(x, g.reshape(1, d), w1, w2, g_final.reshape(1, d))


def _prep_kernel(qn_ref, qp_ref, ckv_ref, kpe_ref, cos_ref, sin_ref, gkv_ref, wuk_ref,
                 q_ref, k_ref, lat_ref, kpeo_ref):
    tm = qn_ref.shape[0]
    cos = cos_ref[...]
    sin = sin_ref[...]
    lane = lax.broadcasted_iota(jnp.int32, (tm, LANES), 1)
    first_half = (lane & (QK_ROPE - 1)) < (QK_ROPE // 2)
    low = lane < QK_ROPE

    def rope(x):
        rot = jnp.where(first_half,
                        pltpu.roll(x, LANES - QK_ROPE // 2, 1),
                        pltpu.roll(x, QK_ROPE // 2, 1))
        return x * cos + rot * sin

    for pair in range(MLA_HEADS // 2):
        xp = rope(qp_ref[:, pair * LANES:(pair + 1) * LANES]) * ATTN_SCALE
        even = jnp.where(low, xp, 0.0)
        odd = jnp.where(low, pltpu.roll(xp, QK_ROPE, 1), 0.0)
        q_ref[2 * pair, :, KV_RANK:QK_WIDTH] = even.astype(BF16)
        q_ref[2 * pair + 1, :, KV_RANK:QK_WIDTH] = odd.astype(BF16)

    for h in range(MLA_HEADS):
        qn = qn_ref[:, h * QK_NOPE:(h + 1) * QK_NOPE].astype(BF16)
        ql = jnp.dot(qn, wuk_ref[h], preferred_element_type=F32) * ATTN_SCALE
        q_ref[h, :, 0:KV_RANK] = ql.astype(BF16)

    ckv = _rms(ckv_ref[...], gkv_ref[...])
    lat_ref[...] = ckv
    k_ref[:, 0:KV_RANK] = ckv.astype(BF16)
    kp = rope(kpe_ref[...])
    kpeo_ref[...] = kp[:, 0:QK_ROPE]
    k_ref[:, KV_RANK:QK_WIDTH] = kp.astype(BF16)


def _prep(proj, cos_tab, sin_tab, g_kv, wuk_t, *, tm):
    t = proj.shape[0]
    n_tab = cos_tab.shape[0] // tm

    def col(width, start):
        return pl.BlockSpec((tm, width), lambda i: (i, start // width))

    return pl.pallas_call(
        _prep_kernel,
        grid=(t // tm,),
        in_specs=[
            col(MLA_HEADS * QK_NOPE, COL_QN),
            col(MLA_HEADS * QK_ROPE, COL_QP),
            col(KV_RANK, COL_CKV),
            col(LANES, COL_KPE),
            pl.BlockSpec((tm, LANES), lambda i: (i % n_tab, 0)),
            pl.BlockSpec((tm, LANES), lambda i: (i % n_tab, 0)),
            pl.BlockSpec((1, KV_RANK), lambda i: (0, 0)),
            pl.BlockSpec((MLA_HEADS, QK_NOPE, KV_RANK), lambda i: (0, 0, 0)),
        ],
        out_specs=[
            pl.BlockSpec((MLA_HEADS, tm, QK_WIDTH), lambda i: (0, i, 0)),
            pl.BlockSpec((tm, QK_WIDTH), lambda i: (i, 0)),
            pl.BlockSpec((tm, KV_RANK), lambda i: (i, 0)),
            pl.BlockSpec((tm, QK_ROPE), lambda i: (i, 0)),
        ],
        out_shape=[
            jax.ShapeDtypeStruct((MLA_HEADS, t, QK_WIDTH), BF16),
            jax.ShapeDtypeStruct((t, QK_WIDTH), BF16),
            jax.ShapeDtypeStruct((t, KV_RANK), F32),
            jax.ShapeDtypeStruct((t, QK_ROPE), F32),
        ],
        compiler_params=_cparams("parallel"),
        name="mla_prep",
    )(proj, proj, proj, proj, cos_tab, sin_tab, g_kv.reshape(1, KV_RANK), wuk_t)


def _flash_kernel(qi_ref, ki_ref, q_ref, k_ref, wuv_ref, o_ref, m_ref, l_ref, acc_ref, *, tq, tk):
    p_id = pl.program_id(1)
    qi = qi_ref[p_id]
    ki = ki_ref[p_id]
    rows = MLA_HEADS * tq

    @pl.when(ki == 0)
    def _():
        m_ref[...] = jnp.full(m_ref.shape, NEG_INF, F32)
        l_ref[...] = jnp.zeros(l_ref.shape, F32)
        acc_ref[...] = jnp.zeros(acc_ref.shape, F32)

    q = q_ref[...].reshape(rows, QK_WIDTH)
    k = k_ref[...]
    s = lax.dot_general(q, k, (((1,), (1,)), ((), ())), preferred_element_type=F32)
    q_pos = qi * tq + (lax.broadcasted_iota(jnp.int32, (rows, tk), 0) & (tq - 1))
    k_pos = ki * tk + lax.broadcasted_iota(jnp.int32, (rows, tk), 1)
    s = jnp.where(k_pos <= q_pos, s, NEG_INF)

    m_prev = m_ref[...]
    m_next = jnp.maximum(m_prev, jnp.max(s, axis=1, keepdims=True))
    alpha = jnp.exp(m_prev - m_next)
    p = jnp.exp(s - _lane_repeat(m_next, tk // LANES))
    l_ref[...] = alpha * l_ref[...] + jnp.sum(p, axis=1, keepdims=True)
    m_ref[...] = m_next
    pv = jnp.dot(p.astype(BF16), k[:, 0:KV_RANK], preferred_element_type=F32)
    acc_ref[...] = acc_ref[...] * _lane_repeat(alpha, KV_RANK // LANES) + pv

    @pl.when((ki + 1) * tk >= (qi + 1) * tq)
    def _():
        inv = 1.0 / l_ref[...]
        o = (acc_ref[...] * _lane_repeat(inv, KV_RANK // LANES)).astype(BF16)
        o = o.reshape(MLA_HEADS, tq, KV_RANK)
        for h in range(MLA_HEADS):
            o_ref[:, h * V_HEAD:(h + 1) * V_HEAD] = jnp.dot(
                o[h], wuv_ref[h], preferred_element_type=F32).astype(BF16)


def _flash_prompt(q, k, wuv_t, *, batch, seq, tq, tk):
    assert tk % tq == 0 and seq % tk == 0
    nq = seq // tq
    qi_list, ki_list = [], []
    for qi in range(nq):
        for ki in range(-(-((qi + 1) * tq) // tk)):
            qi_list.append(qi)
            ki_list.append(ki)
    n_pairs = len(qi_list)
    qi_tab = jnp.asarray(np.asarray(qi_list, np.int32))
    ki_tab = jnp.asarray(np.asarray(ki_list, np.int32))
    nkb = seq // tk
    grid_spec = pltpu.PrefetchScalarGridSpec(
        num_scalar_prefetch=2,
        grid=(batch, n_pairs),
        in_specs=[
            pl.BlockSpec((MLA_HEADS, tq, QK_WIDTH), lambda b, p, qt, kt: (0, b * nq + qt[p], 0)),
            pl.BlockSpec((tk, QK_WIDTH), lambda b, p, qt, kt: (b * nkb + kt[p], 0)),
            pl.BlockSpec((MLA_HEADS, KV_RANK, V_HEAD), lambda b, p, qt, kt: (0, 0, 0)),
        ],
        out_specs=pl.BlockSpec((tq, MLA_HEADS * V_HEAD), lambda b, p, qt, kt: (b * nq + qt[p], 0)),
        scratch_shapes=[
            pltpu.VMEM((MLA_HEADS * tq, LANES), F32),
            pltpu.VMEM((MLA_HEADS * tq, LANES), F32),
            pltpu.VMEM((MLA_HEADS * tq, KV_RANK), F32),
        ],
    )
    return pl.pallas_call(
        functools.partial(_flash_kernel, tq=tq, tk=tk),
        grid_spec=grid_spec,
        out_shape=jax.ShapeDtypeStruct((batch * seq, MLA_HEADS * V_HEAD), BF16),
        compiler_params=_cparams("parallel", "arbitrary"),
        name="flash_prompt",
    )(qi_tab, ki_tab, q, k, wuv_t)


def _expm1(y):
    return jnp.tanh(0.5 * y) * (jnp.exp(y) + 1.0)


def _lru_gates(xc, wa_ref, ba_ref, wx_ref, bx_ref, lam_ref):
    xcb = xc.astype(BF16)
    gr, gi = [], []
    for h in range(LRU_HEADS):
        xh = xcb[:, h * LRU_HEAD_DIM:(h + 1) * LRU_HEAD_DIM]
        gr.append(jnp.dot(xh, wa_ref[h], preferred_element_type=F32))
        gi.append(jnp.dot(xh, wx_ref[h], preferred_element_type=F32))
    gate_r = jnp.concatenate(gr, axis=1) + ba_ref[...]
    gate_i = jnp.concatenate(gi, axis=1) + bx_ref[...]
    r = jax.nn.sigmoid(gate_r)
    i = jax.nn.sigmoid(gate_i)
    log_a = -LRU_C * r * jax.nn.softplus(-lam_ref[...])
    a = jnp.exp(log_a)
    u = jnp.sqrt(-_expm1(2.0 * log_a)) * i * xc
    return a, u


def _lru_prompt_kernel(xb_ref, gb_ref, cw_ref, cb_ref, wa_ref, ba_ref, wx_ref, bx_ref, lam_ref,
                       o_ref, hlast_ref, conv_ref,
                       xpad_ref, a_ref, u_ref, hs_ref, h_ref):
    ts = xb_ref.shape[0]
    t_id = pl.program_id(1)
    halo = 8

    @pl.when(t_id == 0)
    def _():
        xpad_ref[0:halo, :] = jnp.zeros((halo, LRU_WIDTH), F32)
        h_ref[...] = jnp.zeros(h_ref.shape, F32)

    @pl.when(t_id > 0)
    def _():
        xpad_ref[0:halo, :] = xpad_ref[ts:ts + halo, :]

    xpad_ref[halo:halo + ts, :] = xb_ref[...]
    xc = cb_ref[...]
    for kk in range(CONV_W):
        off = halo - (CONV_W - 1) + kk
        xc = xc + xpad_ref[off:off + ts, :] * cw_ref[kk:kk + 1, :]
    conv_ref[0] = xpad_ref[ts + halo - (CONV_W - 1):ts + halo, :]

    a, u = _lru_gates(xc, wa_ref, ba_ref, wx_ref, bx_ref, lam_ref)
    a_ref[...] = a
    u_ref[...] = u

    def step(t, h):
        h = a_ref[pl.ds(t, 1), :] * h + u_ref[pl.ds(t, 1), :]
        hs_ref[pl.ds(t, 1), :] = h
        return h

    h = lax.fori_loop(0, ts, step, h_ref[0:1, :], unroll=8)
    h_ref[0:1, :] = h
    hlast_ref[0] = h
    o_ref[...] = (hs_ref[...] * jax.nn.gelu(gb_ref[...])).astype(BF16)


def _lru_prompt(proj, conv_w, conv_b, w_a, b_a, w_x, b_x, lam, *, batch, seq, ts):
    nt = seq // ts
    w = LRU_WIDTH
    vec = lambda: pl.BlockSpec((1, w), lambda b, t: (0, 0))
    hw = lambda: pl.BlockSpec((LRU_HEADS, LRU_HEAD_DIM, LRU_HEAD_DIM), lambda b, t: (0, 0, 0))
    return pl.pallas_call(
        _lru_prompt_kernel,
        grid=(batch, nt),
        in_specs=[
            pl.BlockSpec((ts, w), lambda b, t: (b * nt + t, COL_XB // w)),
            pl.BlockSpec((ts, w), lambda b, t: (b * nt + t, COL_GB // w)),
            pl.BlockSpec((CONV_W, w), lambda b, t: (0, 0)),
            vec(), hw(), vec(), hw(), vec(), vec(),
        ],
        out_specs=[
            pl.BlockSpec((ts, w), lambda b, t: (b * nt + t, 0)),
            pl.BlockSpec((1, 1, w), lambda b, t: (b, 0, 0)),
            pl.BlockSpec((1, CONV_W - 1, w), lambda b, t: (b, 0, 0)),
        ],
        out_shape=[
            jax.ShapeDtypeStruct((batch * seq, w), BF16),
            jax.ShapeDtypeStruct((batch, 1, w), F32),
            jax.ShapeDtypeStruct((batch, CONV_W - 1, w), F32),
        ],
        scratch_shapes=[
            pltpu.VMEM((ts + 8, w), F32),
            pltpu.VMEM((ts, w), F32),
            pltpu.VMEM((ts, w), F32),
            pltpu.VMEM((ts, w), F32),
            pltpu.VMEM((8, w), F32),
        ],
        compiler_params=_cparams("parallel", "arbitrary"),
        name="lru_prompt",
    )(proj, proj, conv_w, conv_b.reshape(1, w), w_a, b_a.reshape(1, w), w_x, b_x.reshape(1, w),
      lam.reshape(1, w))


def _lru_step_kernel(xb_ref, gb_ref, buf_ref, h0_ref, cw_ref, cb_ref, wa_ref, ba_ref, wx_ref, bx_ref,
                     lam_ref, o_ref, hnew_ref, conv_ref):
    w = LRU_WIDTH
    xb = xb_ref[...]
    xc = cb_ref[...]
    for kk in range(CONV_W - 1):
        xc = xc + buf_ref[:, kk * w:(kk + 1) * w] * cw_ref[kk:kk + 1, :]
    xc = xc + xb * cw_ref[CONV_W - 1:CONV_W, :]
    for kk in range(1, CONV_W - 1):
        conv_ref[:, (kk - 1) * w:kk * w] = buf_ref[:, kk * w:(kk + 1) * w]
    conv_ref[:, (CONV_W - 2) * w:(CONV_W - 1) * w] = xb
    a, u = _lru_gates(xc, wa_ref, ba_ref, wx_ref, bx_ref, lam_ref)
    h = a * h0_ref[...] + u
    hnew_ref[...] = h
    o_ref[...] = (h * jax.nn.gelu(gb_ref[...])).astype(BF16)


def _lru_step(proj, conv_buf, h0, conv_w, conv_b, w_a, b_a, w_x, b_x, lam):
    n = proj.shape[0]
    w = LRU_WIDTH
    full = lambda shape: pl.BlockSpec(shape, lambda i: (0,) * len(shape))
    return pl.pallas_call(
        _lru_step_kernel,
        grid=(1,),
        in_specs=[
            pl.BlockSpec((n, w), lambda i: (0, COL_XB // w)),
            pl.BlockSpec((n, w), lambda i: (0, COL_GB // w)),
            full((n, (CONV_W - 1) * w)), full((n, w)), full((CONV_W, w)), full((1, w)),
            full((LRU_HEADS, LRU_HEAD_DIM, LRU_HEAD_DIM)), full((1, w)),
            full((LRU_HEADS, LRU_HEAD_DIM, LRU_HEAD_DIM)), full((1, w)), full((1, w)),
        ],
        out_specs=[full((n, w)), full((n, w)), full((n, (CONV_W - 1) * w))],
        out_shape=[
            jax.ShapeDtypeStruct((n, w), BF16),
            jax.ShapeDtypeStruct((n, w), F32),
            jax.ShapeDtypeStruct((n, (CONV_W - 1) * w), F32),
        ],
        compiler_params=_cparams("arbitrary"),
        name="lru_step",
    )(proj, proj, conv_buf, h0, conv_w, conv_b.reshape(1, w), w_a, b_a.reshape(1, w), w_x,
      b_x.reshape(1, w), lam.reshape(1, w))


def _paged_kernel(pt_ref, q_ref, kn_ref, lat_hbm, kpe_hbm, o_ref,
                  latbuf, kpebuf, sem, m_ref, l_ref, acc_ref, *, layer, group, n_groups, n_samples):
    b = pl.program_id(0)
    total = n_samples * n_groups
    keys = group * PAGE_SIZE

    def copies(g, slot):
        sb = g // n_groups
        sj = g % n_groups
        out = []
        for i in range(group):
            page = pt_ref[sb, sj * group + i]
            rows = pl.ds(i * PAGE_SIZE, PAGE_SIZE)
            out.append(pltpu.make_async_copy(lat_hbm.at[layer, page], latbuf.at[slot, rows], sem.at[0, slot]))
            out.append(pltpu.make_async_copy(kpe_hbm.at[layer, page], kpebuf.at[slot, rows], sem.at[1, slot]))
        return out

    @pl.when(b == 0)
    def _():
        for c in copies(0, 0):
            c.start()

    m_ref[...] = jnp.full(m_ref.shape, NEG_INF, F32)
    l_ref[...] = jnp.zeros(l_ref.shape, F32)
    acc_ref[...] = jnp.zeros(acc_ref.shape, F32)
    q = q_ref[0]
    q_lat = q[:, 0:KV_RANK]
    q_pe = q[:, KV_RANK:KV_RANK + QK_ROPE]

    def update(s, values_fn):
        m_prev = m_ref[...]
        m_next = jnp.maximum(m_prev, jnp.max(s, axis=1, keepdims=True))
        alpha = jnp.exp(m_prev - m_next)
        p = jnp.exp(s - m_next[:, 0:1])
        l_ref[...] = alpha * l_ref[...] + jnp.sum(p, axis=1, keepdims=True)
        m_ref[...] = m_next
        acc_ref[...] = acc_ref[...] * alpha[:, 0:1] + values_fn(p)

    def body(j, carry):
        g = b * n_groups + j
        slot = g % 2

        @pl.when(g + 1 < total)
        def _():
            for c in copies(g + 1, 1 - slot):
                c.start()

        for c in copies(g, slot):
            c.wait()
        lat = latbuf[slot].astype(BF16)
        kpe = kpebuf[slot].astype(BF16)
        dn = (((1,), (1,)), ((), ()))
        s = lax.dot_general(q_lat, lat, dn, preferred_element_type=F32)
        s = s + lax.dot_general(q_pe, kpe, dn, preferred_element_type=F32)
        update(s, lambda p: jnp.dot(p.astype(BF16), lat, preferred_element_type=F32))
        return carry

    lax.fori_loop(0, n_groups, body, 0)

    kn = kn_ref[0].astype(F32)
    s_new = jnp.sum(q.astype(F32) * kn, axis=1, keepdims=True)
    update(s_new, lambda p: p.astype(BF16).astype(F32) * kn[:, 0:KV_RANK])
    o_ref[0] = acc_ref[...] / l_ref[:, 0:1]


def _paged_attention(page_table, q, k_new, cache_latent, cache_kpe, *, layer, group):
    n, n_pages = page_table.shape
    assert n_pages % group == 0
    n_groups = n_pages // group
    keys = group * PAGE_SIZE
    grid_spec = pltpu.PrefetchScalarGridSpec(
        num_scalar_prefetch=1,
        grid=(n,),
        in_specs=[
            pl.BlockSpec((1, MLA_HEADS, QK_WIDTH), lambda b, pt: (b, 0, 0)),
            pl.BlockSpec((1, 1, QK_WIDTH), lambda b, pt: (b, 0, 0)),
            pl.BlockSpec(memory_space=pl.ANY),
            pl.BlockSpec(memory_space=pl.ANY),
        ],
        out_specs=pl.BlockSpec((1, MLA_HEADS, KV_RANK), lambda b, pt: (b, 0, 0)),
        scratch_shapes=[
            pltpu.VMEM((2, keys, KV_RANK), F32),
            pltpu.VMEM((2, keys, QK_ROPE), F32),
            pltpu.SemaphoreType.DMA((2, 2)),
            pltpu.VMEM((MLA_HEADS, LANES), F32),
            pltpu.VMEM((MLA_HEADS, LANES), F32),
            pltpu.VMEM((MLA_HEADS, KV_RANK), F32),
        ],
    )
    return pl.pallas_call(
        functools.partial(_paged_kernel, layer=layer, group=group, n_groups=n_groups, n_samples=n),
        grid_spec=grid_spec,
        out_shape=jax.ShapeDtypeStruct((n, MLA_HEADS, KV_RANK), F32),
        compiler_params=_cparams("arbitrary"),
        name="paged_attention",
    )(page_table, q, k_new, cache_latent, cache_kpe)


def _value_up_kernel(o_ref, wuv_ref, y_ref):
    for h in range(MLA_HEADS):
        oh = o_ref[:, h * KV_RANK:(h + 1) * KV_RANK].astype(BF16)
        y_ref[:, h * V_HEAD:(h + 1) * V_HEAD] = jnp.dot(
            oh, wuv_ref[h], preferred_element_type=F32).astype(BF16)


def _value_up(o_lat, wuv_t):
    n = o_lat.shape[0]
    full = lambda shape: pl.BlockSpec(shape, lambda i: (0,) * len(shape))
    return pl.pallas_call(
        _value_up_kernel,
        grid=(1,),
        in_specs=[full((n, MLA_HEADS * KV_RANK)), full((MLA_HEADS, KV_RANK, V_HEAD))],
        out_specs=full((n, MLA_HEADS * V_HEAD)),
        out_shape=jax.ShapeDtypeStruct((n, MLA_HEADS * V_HEAD), BF16),
        compiler_params=_cparams("arbitrary"),
        name="value_up",
    )(o_lat.reshape(n, MLA_HEADS * KV_RANK), wuv_t)


def _layer_norm(v, g, b):
    mu = jnp.mean(v, axis=-1, keepdims=True)
    var = jnp.mean(jnp.square(v - mu), axis=-1, keepdims=True)
    return (v - mu) * lax.rsqrt(var + EPS) * g + b


def _gate_prompt_kernel(u_ref, v_ref, g_ref, b_ref, ws_ref, bias_ref, o_ref):
    tm = u_ref.shape[0]
    vb = _layer_norm(v_ref[...], g_ref[...], b_ref[...]).astype(BF16)
    row = lax.broadcasted_iota(jnp.int32, (CHUNK, CHUNK), 0)
    col = lax.broadcasted_iota(jnp.int32, (CHUNK, CHUNK), 1)
    causal = col <= row
    for g in range(GMLP_GROUPS):
        wc = jnp.where(causal, ws_ref[g], 0.0).astype(BF16)
        cs = slice(g * GMLP_GROUP_DIM, (g + 1) * GMLP_GROUP_DIM)
        for c in range(tm // CHUNK):
            rs = slice(c * CHUNK, (c + 1) * CHUNK)
            sg = jnp.dot(wc, vb[rs, cs], preferred_element_type=F32) + bias_ref[:, cs]
            o_ref[rs, cs] = (u_ref[rs, cs] * sg).astype(BF16)


def _gate_prompt(z, ln_g, ln_b, w_s, bias, *, tm):
    t = z.shape[0]
    d = GMLP_HALF
    full = lambda shape: pl.BlockSpec(shape, lambda i: (0,) * len(shape))
    return pl.pallas_call(
        _gate_prompt_kernel,
        grid=(t // tm,),
        in_specs=[
            pl.BlockSpec((tm, d), lambda i: (i, 0)),
            pl.BlockSpec((tm, d), lambda i: (i, 1)),
            full((1, d)), full((1, d)), full((GMLP_GROUPS, CHUNK, CHUNK)), full((CHUNK, d)),
        ],
        out_specs=pl.BlockSpec((tm, d), lambda i: (i, 0)),
        out_shape=jax.ShapeDtypeStruct((t, d), BF16),
        compiler_params=_cparams("parallel"),
        name="gate_prompt",
    )(z, z, ln_g.reshape(1, d), ln_b.reshape(1, d), w_s, bias)


def _gate_step_kernel(u_ref, v_ref, g_ref, b_ref, w0_ref, b0_ref, o_ref, vout_ref):
    v = _layer_norm(v_ref[...], g_ref[...], b_ref[...])
    vout_ref[...] = v
    sg = w0_ref[...].astype(BF16).astype(F32) * v.astype(BF16).astype(F32) + b0_ref[...]
    o_ref[...] = (u_ref[...] * sg).astype(BF16)


def _gate_step(z, ln_g, ln_b, w0, b0):
    n = z.shape[0]
    d = GMLP_HALF
    full = lambda shape: pl.BlockSpec(shape, lambda i: (0,) * len(shape))
    return pl.pallas_call(
        _gate_step_kernel,
        grid=(1,),
        in_specs=[
            pl.BlockSpec((n, d), lambda i: (0, 0)),
            pl.BlockSpec((n, d), lambda i: (0, 1)),
            full((1, d)), full((1, d)), full((1, d)), full((1, d)),
        ],
        out_specs=[full((n, d)), full((n, d))],
        out_shape=[jax.ShapeDtypeStruct((n, d), BF16), jax.ShapeDtypeStruct((n, d), F32)],
        compiler_params=_cparams("arbitrary"),
        name="gate_step",
    )(z, z, ln_g.reshape(1, d), ln_b.reshape(1, d), w0.reshape(1, d), b0.reshape(1, d))


def _matmul_res_kernel(a_ref, w_ref, r_ref, o_ref):
    o_ref[...] = r_ref[...] + jnp.dot(a_ref[...], w_ref[...], preferred_element_type=F32)


def _matmul_res(a, w, res, *, tm, tn):
    t, k = a.shape
    n = w.shape[1]
    return pl.pallas_call(
        _matmul_res_kernel,
        grid=(t // tm, n // tn),
        in_specs=[
            pl.BlockSpec((tm, k), lambda i, j: (i, 0)),
            pl.BlockSpec((k, tn), lambda i, j: (0, j)),
            pl.BlockSpec((tm, tn), lambda i, j: (i, j)),
        ],
        out_specs=pl.BlockSpec((tm, tn), lambda i, j: (i, j)),
        out_shape=jax.ShapeDtypeStruct((t, n), F32),
        compiler_params=_cparams("parallel", "arbitrary"),
        name="matmul_res",
    )(a, w, res)


def _pack_w_in(w_in):
    q_dim = MLA_HEADS * (QK_NOPE + QK_ROPE)
    q = w_in[:, :q_dim].reshape(D_MODEL, MLA_HEADS, QK_NOPE + QK_ROPE)
    qn = q[:, :, :QK_NOPE].reshape(D_MODEL, MLA_HEADS * QK_NOPE)
    qp = q[:, :, QK_NOPE:].reshape(D_MODEL, MLA_HEADS * QK_ROPE)
    ckv = w_in[:, q_dim:q_dim + KV_RANK]
    kpe = w_in[:, q_dim + KV_RANK:q_dim + KV_RANK + QK_ROPE]
    xb = w_in[:, q_dim + KV_RANK + QK_ROPE:q_dim + KV_RANK + QK_ROPE + LRU_WIDTH]
    gb = w_in[:, q_dim + KV_RANK + QK_ROPE + LRU_WIDTH:]
    pad = jnp.zeros((D_MODEL, LANES - QK_ROPE), w_in.dtype)
    packed = jnp.concatenate([qn, xb, gb, qp, ckv, kpe, pad], axis=1)
    assert packed.shape[1] == PROJ_WIDTH
    return packed.astype(BF16)


def _rope_tables(pos):
    half = QK_ROPE // 2
    inv = ROPE_THETA ** (-jnp.arange(half, dtype=F32) / half)
    ang = pos.astype(F32)[:, None] * inv[None, :]
    cos = jnp.cos(ang)
    sin = jnp.sin(ang)
    reps = LANES // QK_ROPE
    cos_tab = jnp.tile(jnp.concatenate([cos, cos], axis=1), (1, reps))
    sin_tab = jnp.tile(jnp.concatenate([-sin, sin], axis=1), (1, reps))
    return cos_tab, sin_tab


class _Plan(NamedTuple):
    tm_prompt: int
    tm_sample: int
    t_small: int
    tq: int
    tk: int
    page_group: int
    tf: int


def _plan(batch, seq, n_dec, n_pages):
    t = batch * seq
    tm_prompt = 1024 if t % 1024 == 0 else t
    t_small = 512 if seq % 512 == 0 else seq
    tk = 512 if seq % 512 == 0 else seq
    return _Plan(tm_prompt=tm_prompt, tm_sample=n_dec, t_small=t_small, tq=min(256, tk), tk=tk,
                 page_group=8 if n_pages % 8 == 0 else 1, tf=512)


def kernel(x_prompt, x_sample, cache_latent, cache_kpe, state_lru_h, state_conv, page_table,
           norm_mix, norm_ffn, norm_final, ab_w_in, ab_g_kv, ab_w_uk, ab_w_uv, ab_conv_w, ab_conv_b,
           ab_w_a, ab_b_a, ab_w_x, ab_b_x, ab_lambda, ab_w_out, c_w_in, c_ln_g, c_ln_b, c_w_s, c_b_s,
           c_w_out, ffn_w1, ffn_w2):
    batch, seq, _ = x_prompt.shape
    n_dec, dec_seq, _ = x_sample.shape
    assert dec_seq == 1
    n_pages = page_table.shape[1]
    past_len = n_pages * PAGE_SIZE
    depth = norm_mix.shape[0]

    cos_p, sin_p = _rope_tables(jnp.arange(seq, dtype=jnp.int32))
    cos_s, sin_s = _rope_tables(jnp.full((n_dec,), past_len, jnp.int32))

    hp = x_prompt.reshape(batch * seq, D_MODEL)
    hs = x_sample.reshape(n_dec, D_MODEL)
    plan = _plan(batch, seq, n_dec, n_pages)
    tp = plan.tm_prompt
    tsm = plan.tm_sample

    lat_p, kpe_p, lru_p, conv_p = [], [], [], []
    lat_s, kpe_s, lru_s, conv_s, v_s = [], [], [], [], []

    for l in range(depth):
        j = l // 2
        if l % 2 == 0:
            w_in = _pack_w_in(ab_w_in[j])
            wuk_t = jnp.transpose(ab_w_uk[j], (1, 2, 0)).astype(BF16)
            wuv_t = jnp.transpose(ab_w_uv[j], (1, 0, 2)).astype(BF16)
            w_a = ab_w_a[j].astype(BF16)
            w_x = ab_w_x[j].astype(BF16)
            w_out = ab_w_out[j].astype(BF16)
            lru_args = (ab_conv_w[j], ab_conv_b[j], w_a, ab_b_a[j], w_x, ab_b_x[j], ab_lambda[j])

            proj = _norm_matmul(hp, norm_mix[l], w_in, tm=tp, tn=PROJ_WIDTH // 3)
            q, k, lat, kpe = _prep(proj, cos_p, sin_p, ab_g_kv[j], wuk_t, tm=plan.t_small)
            o_mla = _flash_prompt(q, k, wuv_t, batch=batch, seq=seq, tq=plan.tq, tk=plan.tk)
            o_lru, h_last, conv_new = _lru_prompt(proj, *lru_args, batch=batch, seq=seq, ts=plan.t_small)
            hp = _pair_matmul_res(o_mla, o_lru, w_out, hp, tm=tp, tn=1024)
            lat_p.append(lat.reshape(batch, seq, KV_RANK))
            kpe_p.append(kpe.reshape(batch, seq, QK_ROPE))
            lru_p.append(h_last.reshape(batch, LRU_WIDTH))
            conv_p.append(conv_new)

            proj = _norm_matmul(hs, norm_mix[l], w_in, tm=tsm, tn=PROJ_WIDTH // 3)
            q, k, lat, kpe = _prep(proj, cos_s, sin_s, ab_g_kv[j], wuk_t, tm=tsm)
            o_lat = _paged_attention(page_table, jnp.transpose(q, (1, 0, 2)),
                                     k.reshape(n_dec, 1, QK_WIDTH), cache_latent, cache_kpe,
                                     layer=j, group=plan.page_group)
            o_mla = _value_up(o_lat, wuv_t)
            o_lru, h_new, conv_new = _lru_step(
                proj, state_conv[j].reshape(n_dec, (CONV_W - 1) * LRU_WIDTH), state_lru_h[j], *lru_args)
            hs = _pair_matmul_res(o_mla, o_lru, w_out, hs, tm=tsm, tn=1024)
            lat_s.append(lat.reshape(n_dec, 1, KV_RANK))
            kpe_s.append(kpe.reshape(n_dec, 1, QK_ROPE))
            lru_s.append(h_new)
            conv_s.append(conv_new.reshape(n_dec, CONV_W - 1, LRU_WIDTH))
        else:
            w_in = c_w_in[j].astype(BF16)
            w_out = c_w_out[j].astype(BF16)
            bias = jnp.repeat(c_b_s[j].T, GMLP_GROUP_DIM, axis=1)
            w0 = jnp.repeat(c_w_s[j][:, 0, 0], GMLP_GROUP_DIM)

            z = _norm_matmul(hp, norm_mix[l], w_in, tm=tp, tn=1024, act="gelu")
            a = _gate_prompt(z, c_ln_g[j], c_ln_b[j], c_w_s[j], bias, tm=plan.t_small)
            hp = _matmul_res(a, w_out, hp, tm=tp, tn=1024)

            z = _norm_matmul(hs, norm_mix[l], w_in, tm=tsm, tn=1024, act="gelu")
            a, v = _gate_step(z, c_ln_g[j], c_ln_b[j], w0, bias[0])
            hs = _matmul_res(a, w_out, hs, tm=tsm, tn=1024)
            v_s.append(v.reshape(n_dec, 1, GMLP_HALF))

        w1 = ffn_w1[l].astype(BF16)
        w2 = ffn_w2[l].astype(BF16)
        last = l == depth - 1
        hp = _ffn(hp, norm_ffn[l], w1, w2, norm_final, tm=tp, tf=plan.tf, final_norm=last)
        hs = _ffn(hs, norm_ffn[l], w1, w2, norm_final, tm=tsm, tf=plan.tf, final_norm=last)

    return (hp.reshape(batch, seq, D_MODEL), hs.reshape(n_dec, 1, D_MODEL),
            jnp.stack(lat_p), jnp.stack(kpe_p), jnp.stack(lru_p), jnp.stack(conv_p),
            jnp.stack(lat_s), jnp.stack(kpe_s), jnp.stack(lru_s), jnp.stack(conv_s), jnp.stack(v_s))
```

```python
import functools
from typing import NamedTuple

import numpy as np
import jax
import jax.numpy as jnp
from jax import lax
from jax.experimental import pallas as pl
from jax.experimental.pallas import tpu as pltpu

D_MODEL = 2048
MLA_HEADS = 8
QK_NOPE = 128
QK_ROPE = 64
V_HEAD = 128
KV_RANK = 512
ROPE_THETA = 10000.0
ATTN_SCALE = (QK_NOPE + QK_ROPE) ** -0.5
LRU_WIDTH = D_MODEL // 2
LRU_HEADS = 8
LRU_HEAD_DIM = LRU_WIDTH // LRU_HEADS
CONV_W = 4
LRU_C = 8.0
GMLP_HALF = D_MODEL
CHUNK = 128
GMLP_GROUPS = 8
GMLP_GROUP_DIM = GMLP_HALF // GMLP_GROUPS
D_FF = 4 * D_MODEL
EPS = 1e-6
NEG_INF = -1e30
PAGE_SIZE = 128

LANES = 128
V7X_VMEM_BYTES = 64 * 1024 * 1024
VMEM_LIMIT = (V7X_VMEM_BYTES * 7) // 8

PAGE_BUFFERS = 3

QK_WIDTH = KV_RANK + LANES

COL_QN = 0
COL_XB = 1024
COL_GB = 2048
COL_QP = 3072
COL_CKV = 3584
COL_KPE = 4096
PROJ_WIDTH = 4224

BF16 = jnp.bfloat16
F32 = jnp.float32


def _cparams(*sem):
    return pltpu.CompilerParams(dimension_semantics=sem, vmem_limit_bytes=VMEM_LIMIT)


def _lane_repeat(x, n):
    return jnp.concatenate([x] * n, axis=1)


def _rms(x, g):
    return x * lax.rsqrt(jnp.mean(x * x, axis=-1, keepdims=True) + EPS) * g


def _norm_matmul_kernel(x_ref, g_ref, w_ref, o_ref, xn_ref, *, act):
    @pl.when(pl.program_id(1) == 0)
    def _():
        xn_ref[...] = _rms(x_ref[...], g_ref[...]).astype(BF16)

    y = jnp.dot(xn_ref[...], w_ref[...], preferred_element_type=F32)
    if act == "gelu":
        y = jax.nn.gelu(y)
    o_ref[...] = y


def _norm_matmul(x, g, w, *, tm, tn, act=None):
    t, k = x.shape
    n = w.shape[1]
    return pl.pallas_call(
        functools.partial(_norm_matmul_kernel, act=act),
        grid=(t // tm, n // tn),
        in_specs=[
            pl.BlockSpec((tm, k), lambda i, j: (i, 0)),
            pl.BlockSpec((1, k), lambda i, j: (0, 0)),
            pl.BlockSpec((k, tn), lambda i, j: (0, j)),
        ],
        out_specs=pl.BlockSpec((tm, tn), lambda i, j: (i, j)),
        out_shape=jax.ShapeDtypeStruct((t, n), F32),
        scratch_shapes=[pltpu.VMEM((tm, k), BF16)],
        compiler_params=_cparams("parallel", "arbitrary"),
        name="norm_matmul",
    )(x, g.reshape(1, k), w)


def _pair_matmul_res_kernel(a_ref, b_ref, wa_ref, wb_ref, r_ref, o_ref):
    y = jnp.dot(a_ref[...], wa_ref[...], preferred_element_type=F32)
    y = y + jnp.dot(b_ref[...], wb_ref[...], preferred_element_type=F32)
    o_ref[...] = r_ref[...] + y


def _pair_matmul_res(a, b, w, res, *, tm, tn):
    t, ka = a.shape
    kb = b.shape[1]
    assert ka == kb and w.shape[0] == ka + kb
    n = w.shape[1]
    return pl.pallas_call(
        _pair_matmul_res_kernel,
        grid=(t // tm, n // tn),
        in_specs=[
            pl.BlockSpec((tm, ka), lambda i, j: (i, 0)),
            pl.BlockSpec((tm, kb), lambda i, j: (i, 0)),
            pl.BlockSpec((ka, tn), lambda i, j: (0, j)),
            pl.BlockSpec((kb, tn), lambda i, j: (1, j)),
            pl.BlockSpec((tm, tn), lambda i, j: (i, j)),
        ],
        out_specs=pl.BlockSpec((tm, tn), lambda i, j: (i, j)),
        out_shape=jax.ShapeDtypeStruct((t, n), F32),
        compiler_params=_cparams("parallel", "arbitrary"),
        name="pair_matmul_res",
    )(a, b, w, w, res)


def _ffn_kernel(x_ref, g_ref, w1_ref, w2_ref, gf_ref, o_ref, xn_ref, *, final_norm, col_chunk):
    j = pl.program_id(1)

    @pl.when(j == 0)
    def _():
        x = x_ref[...]
        xn_ref[...] = _rms(x, g_ref[...]).astype(BF16)
        o_ref[...] = x

    a = jnp.dot(xn_ref[...], w1_ref[...], preferred_element_type=F32)
    a = jnp.square(jnp.maximum(a, 0.0)).astype(BF16)
    for c in range(o_ref.shape[1] // col_chunk):
        sl = slice(c * col_chunk, (c + 1) * col_chunk)
        o_ref[:, sl] += jnp.dot(a, w2_ref[:, sl], preferred_element_type=F32)

    if final_norm:
        @pl.when(j == pl.num_programs(1) - 1)
        def _():
            o_ref[...] = _rms(o_ref[...], gf_ref[...])


def _ffn(x, g, w1, w2, g_final, *, layer, tm, tf, final_norm):
    t, d = x.shape
    f = w1.shape[2]
    return pl.pallas_call(
        functools.partial(_ffn_kernel, final_norm=final_norm, col_chunk=512),
        grid=(t // tm, f // tf),
        in_specs=[
            pl.BlockSpec((tm, d), lambda i, j: (i, 0)),
            pl.BlockSpec((1, d), lambda i, j: (0, 0)),
            pl.BlockSpec((None, d, tf), lambda i, j: (layer, 0, j)),
            pl.BlockSpec((None, tf, d), lambda i, j: (layer, j, 0)),
            pl.BlockSpec((1, d), lambda i, j: (0, 0)),
        ],
        out_specs=pl.BlockSpec((tm, d), lambda i, j: (i, 0)),
        out_shape=jax.ShapeDtypeStruct((t, d), F32),
        scratch_shapes=[pltpu.VMEM((tm, d), BF16)],
        compiler_params=_cparams("parallel", "arbitrary"),
        name="ffn",
    )(x, g.reshape(1, d), w1, w2, g_final.reshape(1, d))


def _prep_kernel(qn_ref, qp_ref, ckv_ref, kpe_ref, cos_ref, sin_ref, gkv_ref, wuk_ref,
                 q_ref, k_ref, lat_ref, kpeo_ref):
    tm = qn_ref.shape[0]
    cos = cos_ref[...]
    sin = sin_ref[...]
    lane = lax.broadcasted_iota(jnp.int32, (tm, LANES), 1)
    first_half = (lane & (QK_ROPE - 1)) < (QK_ROPE // 2)
    low = lane < QK_ROPE

    def rope(x):
        rot = jnp.where(first_half,
                        pltpu.roll(x, LANES - QK_ROPE // 2, 1),
                        pltpu.roll(x, QK_ROPE // 2, 1))
        return x * cos + rot * sin

    for pair in range(MLA_HEADS // 2):
        xp = rope(qp_ref[:, pair * LANES:(pair + 1) * LANES]) * ATTN_SCALE
        even = jnp.where(low, xp, 0.0)
        odd = jnp.where(low, pltpu.roll(xp, QK_ROPE, 1), 0.0)
        q_ref[2 * pair, :, KV_RANK:QK_WIDTH] = even.astype(BF16)
        q_ref[2 * pair + 1, :, KV_RANK:QK_WIDTH] = odd.astype(BF16)

    for h in range(MLA_HEADS):
        qn = qn_ref[:, h * QK_NOPE:(h + 1) * QK_NOPE].astype(BF16)
        ql = jnp.dot(qn, wuk_ref[h], preferred_element_type=F32) * ATTN_SCALE
        q_ref[h, :, 0:KV_RANK] = ql.astype(BF16)

    ckv = _rms(ckv_ref[...], gkv_ref[...])
    lat_ref[...] = ckv
    k_ref[:, 0:KV_RANK] = ckv.astype(BF16)
    kp = rope(kpe_ref[...])
    kpeo_ref[...] = kp[:, 0:QK_ROPE]
    k_ref[:, KV_RANK:QK_WIDTH] = kp.astype(BF16)


def _prep(proj, cos_tab, sin_tab, g_kv, wuk_t, *, tm):
    t = proj.shape[0]
    n_tab = cos_tab.shape[0] // tm

    def col(width, start):
        return pl.BlockSpec((tm, width), lambda i: (i, start // width))

    return pl.pallas_call(
        _prep_kernel,
        grid=(t // tm,),
        in_specs=[
            col(MLA_HEADS * QK_NOPE, COL_QN),
            col(MLA_HEADS * QK_ROPE, COL_QP),
            col(KV_RANK, COL_CKV),
            col(LANES, COL_KPE),
            pl.BlockSpec((tm, LANES), lambda i: (i % n_tab, 0)),
            pl.BlockSpec((tm, LANES), lambda i: (i % n_tab, 0)),
            pl.BlockSpec((1, KV_RANK), lambda i: (0, 0)),
            pl.BlockSpec((MLA_HEADS, QK_NOPE, KV_RANK), lambda i: (0, 0, 0)),
        ],
        out_specs=[
            pl.BlockSpec((MLA_HEADS, tm, QK_WIDTH), lambda i: (0, i, 0)),
            pl.BlockSpec((tm, QK_WIDTH), lambda i: (i, 0)),
            pl.BlockSpec((tm, KV_RANK), lambda i: (i, 0)),
            pl.BlockSpec((tm, QK_ROPE), lambda i: (i, 0)),
        ],
        out_shape=[
            jax.ShapeDtypeStruct((MLA_HEADS, t, QK_WIDTH), BF16),
            jax.ShapeDtypeStruct((t, QK_WIDTH), BF16),
            jax.ShapeDtypeStruct((t, KV_RANK), F32),
            jax.ShapeDtypeStruct((t, QK_ROPE), F32),
        ],
        compiler_params=_cparams("parallel"),
        name="mla_prep",
    )(proj, proj, proj, proj, cos_tab, sin_tab, g_kv.reshape(1, KV_RANK), wuk_t)


def _flash_kernel(qi_ref, ki_ref, q_ref, k_ref, wuv_ref, o_ref, m_ref, l_ref, acc_ref, *, tq, tk):
    p_id = pl.program_id(1)
    qi = qi_ref[p_id]
    ki = ki_ref[p_id]
    rows = MLA_HEADS * tq

    @pl.when(ki == 0)
    def _():
        m_ref[...] = jnp.full(m_ref.shape, NEG_INF, F32)
        l_ref[...] = jnp.zeros(l_ref.shape, F32)
        acc_ref[...] = jnp.zeros(acc_ref.shape, F32)

    q = q_ref[...].reshape(rows, QK_WIDTH)
    k = k_ref[...]
    s = lax.dot_general(q, k, (((1,), (1,)), ((), ())), preferred_element_type=F32)
    q_pos = qi * tq + (lax.broadcasted_iota(jnp.int32, (rows, tk), 0) & (tq - 1))
    k_pos = ki * tk + lax.broadcasted_iota(jnp.int32, (rows, tk), 1)
    s = jnp.where(k_pos <= q_pos, s, NEG_INF)

    m_prev = m_ref[...]
    m_next = jnp.maximum(m_prev, jnp.max(s, axis=1, keepdims=True))
    alpha = jnp.exp(m_prev - m_next)
    p = jnp.exp(s - _lane_repeat(m_next, tk // LANES))
    l_ref[...] = alpha * l_ref[...] + jnp.sum(p, axis=1, keepdims=True)
    m_ref[...] = m_next
    pv = jnp.dot(p.astype(BF16), k[:, 0:KV_RANK], preferred_element_type=F32)
    acc_ref[...] = acc_ref[...] * _lane_repeat(alpha, KV_RANK // LANES) + pv

    @pl.when((ki + 1) * tk >= (qi + 1) * tq)
    def _():
        inv = 1.0 / l_ref[...]
        o = (acc_ref[...] * _lane_repeat(inv, KV_RANK // LANES)).astype(BF16)
        o = o.reshape(MLA_HEADS, tq, KV_RANK)
        for h in range(MLA_HEADS):
            o_ref[:, h * V_HEAD:(h + 1) * V_HEAD] = jnp.dot(
                o[h], wuv_ref[h], preferred_element_type=F32).astype(BF16)


def _flash_prompt(q, k, wuv_t, *, batch, seq, tq, tk):
    assert tk % tq == 0 and seq % tk == 0
    nq = seq // tq
    qi_list, ki_list = [], []
    for qi in range(nq):
        for ki in range(-(-((qi + 1) * tq) // tk)):
            qi_list.append(qi)
            ki_list.append(ki)
    n_pairs = len(qi_list)
    qi_tab = jnp.asarray(np.asarray(qi_list, np.int32))
    ki_tab = jnp.asarray(np.asarray(ki_list, np.int32))
    nkb = seq // tk
    grid_spec = pltpu.PrefetchScalarGridSpec(
        num_scalar_prefetch=2,
        grid=(batch, n_pairs),
        in_specs=[
            pl.BlockSpec((MLA_HEADS, tq, QK_WIDTH), lambda b, p, qt, kt: (0, b * nq + qt[p], 0)),
            pl.BlockSpec((tk, QK_WIDTH), lambda b, p, qt, kt: (b * nkb + kt[p], 0)),
            pl.BlockSpec((MLA_HEADS, KV_RANK, V_HEAD), lambda b, p, qt, kt: (0, 0, 0)),
        ],
        out_specs=pl.BlockSpec((tq, MLA_HEADS * V_HEAD), lambda b, p, qt, kt: (b * nq + qt[p], 0)),
        scratch_shapes=[
            pltpu.VMEM((MLA_HEADS * tq, LANES), F32),
            pltpu.VMEM((MLA_HEADS * tq, LANES), F32),
            pltpu.VMEM((MLA_HEADS * tq, KV_RANK), F32),
        ],
    )
    return pl.pallas_call(
        functools.partial(_flash_kernel, tq=tq, tk=tk),
        grid_spec=grid_spec,
        out_shape=jax.ShapeDtypeStruct((batch * seq, MLA_HEADS * V_HEAD), BF16),
        compiler_params=_cparams("parallel", "arbitrary"),
        name="flash_prompt",
    )(qi_tab, ki_tab, q, k, wuv_t)


def _expm1(y):
    return jnp.tanh(0.5 * y) * (jnp.exp(y) + 1.0)


def _lru_gates(xc, wa_ref, ba_ref, wx_ref, bx_ref, lam_ref):
    xcb = xc.astype(BF16)
    gr, gi = [], []
    for h in range(LRU_HEADS):
        xh = xcb[:, h * LRU_HEAD_DIM:(h + 1) * LRU_HEAD_DIM]
        gr.append(jnp.dot(xh, wa_ref[h], preferred_element_type=F32))
        gi.append(jnp.dot(xh, wx_ref[h], preferred_element_type=F32))
    gate_r = jnp.concatenate(gr, axis=1) + ba_ref[...]
    gate_i = jnp.concatenate(gi, axis=1) + bx_ref[...]
    r = jax.nn.sigmoid(gate_r)
    i = jax.nn.sigmoid(gate_i)
    log_a = -LRU_C * r * jax.nn.softplus(-lam_ref[...])
    a = jnp.exp(log_a)
    u = jnp.sqrt(-_expm1(2.0 * log_a)) * i * xc
    return a, u


def _lru_prompt_kernel(xb_ref, gb_ref, cw_ref, cb_ref, wa_ref, ba_ref, wx_ref, bx_ref, lam_ref,
                       o_ref, hlast_ref, conv_ref,
                       xpad_ref, a_ref, u_ref, hs_ref, h_ref):
    ts = xb_ref.shape[0]
    t_id = pl.program_id(1)
    halo = 8

    @pl.when(t_id == 0)
    def _():
        xpad_ref[0:halo, :] = jnp.zeros((halo, LRU_WIDTH), F32)
        h_ref[...] = jnp.zeros(h_ref.shape, F32)

    @pl.when(t_id > 0)
    def _():
        xpad_ref[0:halo, :] = xpad_ref[ts:ts + halo, :]

    xpad_ref[halo:halo + ts, :] = xb_ref[...]
    xc = cb_ref[...]
    for kk in range(CONV_W):
        off = halo - (CONV_W - 1) + kk
        xc = xc + xpad_ref[off:off + ts, :] * cw_ref[kk:kk + 1, :]
    conv_ref[0] = xpad_ref[ts + halo - (CONV_W - 1):ts + halo, :]

    a, u = _lru_gates(xc, wa_ref, ba_ref, wx_ref, bx_ref, lam_ref)
    a_ref[...] = a
    u_ref[...] = u

    def step(t, h):
        h = a_ref[pl.ds(t, 1), :] * h + u_ref[pl.ds(t, 1), :]
        hs_ref[pl.ds(t, 1), :] = h
        return h

    h = lax.fori_loop(0, ts, step, h_ref[0:1, :], unroll=8)
    h_ref[0:1, :] = h
    hlast_ref[0] = h
    o_ref[...] = (hs_ref[...] * jax.nn.gelu(gb_ref[...])).astype(BF16)


def _lru_prompt(proj, conv_w, conv_b, w_a, b_a, w_x, b_x, lam, *, batch, seq, ts):
    nt = seq // ts
    w = LRU_WIDTH
    vec = lambda: pl.BlockSpec((1, w), lambda b, t: (0, 0))
    hw = lambda: pl.BlockSpec((LRU_HEADS, LRU_HEAD_DIM, LRU_HEAD_DIM), lambda b, t: (0, 0, 0))
    return pl.pallas_call(
        _lru_prompt_kernel,
        grid=(batch, nt),
        in_specs=[
            pl.BlockSpec((ts, w), lambda b, t: (b * nt + t, COL_XB // w)),
            pl.BlockSpec((ts, w), lambda b, t: (b * nt + t, COL_GB // w)),
            pl.BlockSpec((CONV_W, w), lambda b, t: (0, 0)),
            vec(), hw(), vec(), hw(), vec(), vec(),
        ],
        out_specs=[
            pl.BlockSpec((ts, w), lambda b, t: (b * nt + t, 0)),
            pl.BlockSpec((1, 1, w), lambda b, t: (b, 0, 0)),
            pl.BlockSpec((1, CONV_W - 1, w), lambda b, t: (b, 0, 0)),
        ],
        out_shape=[
            jax.ShapeDtypeStruct((batch * seq, w), BF16),
            jax.ShapeDtypeStruct((batch, 1, w), F32),
            jax.ShapeDtypeStruct((batch, CONV_W - 1, w), F32),
        ],
        scratch_shapes=[
            pltpu.VMEM((ts + 8, w), F32),
            pltpu.VMEM((ts, w), F32),
            pltpu.VMEM((ts, w), F32),
            pltpu.VMEM((ts, w), F32),
            pltpu.VMEM((8, w), F32),
        ],
        compiler_params=_cparams("parallel", "arbitrary"),
        name="lru_prompt",
    )(proj, proj, conv_w, conv_b.reshape(1, w), w_a, b_a.reshape(1, w), w_x, b_x.reshape(1, w),
      lam.reshape(1, w))


def _lru_step_kernel(xb_ref, gb_ref, buf_ref, h0_ref, cw_ref, cb_ref, wa_ref, ba_ref, wx_ref, bx_ref,
                     lam_ref, o_ref, hnew_ref, conv_ref):
    w = LRU_WIDTH
    xb = xb_ref[...]
    xc = cb_ref[...]
    for kk in range(CONV_W - 1):
        xc = xc + buf_ref[:, kk * w:(kk + 1) * w] * cw_ref[kk:kk + 1, :]
    xc = xc + xb * cw_ref[CONV_W - 1:CONV_W, :]
    for kk in range(1, CONV_W - 1):
        conv_ref[:, (kk - 1) * w:kk * w] = buf_ref[:, kk * w:(kk + 1) * w]
    conv_ref[:, (CONV_W - 2) * w:(CONV_W - 1) * w] = xb
    a, u = _lru_gates(xc, wa_ref, ba_ref, wx_ref, bx_ref, lam_ref)
    h = a * h0_ref[...] + u
    hnew_ref[...] = h
    o_ref[...] = (h * jax.nn.gelu(gb_ref[...])).astype(BF16)


def _lru_step(proj, conv_buf, h0, conv_w, conv_b, w_a, b_a, w_x, b_x, lam):
    n = proj.shape[0]
    w = LRU_WIDTH
    full = lambda shape: pl.BlockSpec(shape, lambda i: (0,) * len(shape))
    return pl.pallas_call(
        _lru_step_kernel,
        grid=(1,),
        in_specs=[
            pl.BlockSpec((n, w), lambda i: (0, COL_XB // w)),
            pl.BlockSpec((n, w), lambda i: (0, COL_GB // w)),
            full((n, (CONV_W - 1) * w)), full((n, w)), full((CONV_W, w)), full((1, w)),
            full((LRU_HEADS, LRU_HEAD_DIM, LRU_HEAD_DIM)), full((1, w)),
            full((LRU_HEADS, LRU_HEAD_DIM, LRU_HEAD_DIM)), full((1, w)), full((1, w)),
        ],
        out_specs=[full((n, w)), full((n, w)), full((n, (CONV_W - 1) * w))],
        out_shape=[
            jax.ShapeDtypeStruct((n, w), BF16),
            jax.ShapeDtypeStruct((n, w), F32),
            jax.ShapeDtypeStruct((n, (CONV_W - 1) * w), F32),
        ],
        compiler_params=_cparams("arbitrary"),
        name="lru_step",
    )(proj, proj, conv_buf, h0, conv_w, conv_b.reshape(1, w), w_a, b_a.reshape(1, w), w_x,
      b_x.reshape(1, w), lam.reshape(1, w))


def _paged_kernel(pt_ref, q_ref, kn_ref, lat_hbm, kpe_hbm, o_ref,
                  latbuf, kpebuf, sem, latb_ref, s_ref, m_ref, l_ref, acc_ref,
                  *, layer, group, n_groups, n_samples):
    b = pl.program_id(0)
    total = n_samples * n_groups
    n_buf = latbuf.shape[0]

    def copies(g, slot):
        gc = jnp.minimum(g, total - 1)
        sb = gc // n_groups
        sj = gc % n_groups
        out = []
        for i in range(group):
            page = pt_ref[sb, sj * group + i]
            keys = pl.ds(i * PAGE_SIZE, PAGE_SIZE)
            out.append(pltpu.make_async_copy(lat_hbm.at[layer, page], latbuf.at[slot, keys], sem.at[0, slot]))
            out.append(pltpu.make_async_copy(kpe_hbm.at[layer, page], kpebuf.at[slot, :, keys], sem.at[1, slot]))
        return out

    def wait_group(g):
        for c in copies(g, g % n_buf):
            c.wait()

    def stage_scores(g, parity):
        slot = g % n_buf
        q = q_ref[jnp.minimum(g, total - 1) // n_groups]
        lat = latbuf[slot].astype(BF16)
        latb_ref[parity] = lat
        kpe = kpebuf[slot].astype(BF16)
        s = lax.dot_general(q[:, 0:KV_RANK], lat, (((1,), (1,)), ((), ())), preferred_element_type=F32)
        s = s + jnp.dot(q[:, KV_RANK:KV_RANK + QK_ROPE], kpe, preferred_element_type=F32)
        s_ref[parity] = s

    def softmax_step(s):
        m_prev = m_ref[...]
        m_next = jnp.maximum(m_prev, jnp.max(s, axis=1, keepdims=True))
        alpha = jnp.exp(m_prev - m_next)
        p = jnp.exp(s - m_next[:, 0:1])
        l_ref[...] = alpha * l_ref[...] + jnp.sum(p, axis=1, keepdims=True)
        m_ref[...] = m_next
        return p, alpha[:, 0:1]

    @pl.when(b == 0)
    def _():
        for g0 in range(n_buf):
            for c in copies(g0, g0):
                c.start()
        wait_group(0)
        stage_scores(0, 0)

    m_ref[...] = jnp.full(m_ref.shape, NEG_INF, F32)
    l_ref[...] = jnp.zeros(l_ref.shape, F32)
    acc_ref[...] = jnp.zeros(acc_ref.shape, F32)

    def body(jj, carry):
        for parity in range(2):
            g = b * n_groups + 2 * jj + parity
            wait_group(g + 1)

            @pl.when(g + n_buf <= total)
            def _():
                for c in copies(g + n_buf, g % n_buf):
                    c.start()

            p, alpha = softmax_step(s_ref[parity])
            stage_scores(g + 1, 1 - parity)
            pv = jnp.dot(p.astype(BF16), latb_ref[parity], preferred_element_type=F32)
            acc_ref[...] = acc_ref[...] * alpha + pv
        return carry

    lax.fori_loop(0, n_groups // 2, body, 0)

    q = q_ref[b]
    kn = kn_ref[0].astype(F32)
    s_new = jnp.sum(q.astype(F32) * kn, axis=1, keepdims=True)
    p, alpha = softmax_step(s_new)
    acc = acc_ref[...] * alpha + p.astype(BF16).astype(F32) * kn[:, 0:KV_RANK]
    o_ref[0] = acc / l_ref[:, 0:1]


def _paged_attention(page_table, q, k_new, cache_latent, cache_kpe_t, *, layer, group):
    n, n_pages = page_table.shape
    assert n_pages % group == 0
    n_groups = n_pages // group
    assert n_groups % 2 == 0 and n * n_groups >= PAGE_BUFFERS
    keys = group * PAGE_SIZE
    grid_spec = pltpu.PrefetchScalarGridSpec(
        num_scalar_prefetch=1,
        grid=(n,),
        in_specs=[
            pl.BlockSpec((n, MLA_HEADS, QK_WIDTH), lambda b, pt: (0, 0, 0)),
            pl.BlockSpec((1, 1, QK_WIDTH), lambda b, pt: (b, 0, 0)),
            pl.BlockSpec(memory_space=pl.ANY),
            pl.BlockSpec(memory_space=pl.ANY),
        ],
        out_specs=pl.BlockSpec((1, MLA_HEADS, KV_RANK), lambda b, pt: (b, 0, 0)),
        scratch_shapes=[
            pltpu.VMEM((PAGE_BUFFERS, keys, KV_RANK), F32),
            pltpu.VMEM((PAGE_BUFFERS, QK_ROPE, keys), F32),
            pltpu.SemaphoreType.DMA((2, PAGE_BUFFERS)),
            pltpu.VMEM((2, keys, KV_RANK), BF16),
            pltpu.VMEM((2, MLA_HEADS, keys), F32),
            pltpu.VMEM((MLA_HEADS, LANES), F32),
            pltpu.VMEM((MLA_HEADS, LANES), F32),
            pltpu.VMEM((MLA_HEADS, KV_RANK), F32),
        ],
    )
    return pl.pallas_call(
        functools.partial(_paged_kernel, layer=layer, group=group, n_groups=n_groups, n_samples=n),
        grid_spec=grid_spec,
        out_shape=jax.ShapeDtypeStruct((n, MLA_HEADS, KV_RANK), F32),
        compiler_params=_cparams("arbitrary"),
        name="paged_attention",
    )(page_table, q, k_new, cache_latent, cache_kpe_t)


def _value_up_kernel(o_ref, wuv_ref, y_ref):
    for h in range(MLA_HEADS):
        oh = o_ref[:, h * KV_RANK:(h + 1) * KV_RANK].astype(BF16)
        y_ref[:, h * V_HEAD:(h + 1) * V_HEAD] = jnp.dot(
            oh, wuv_ref[h], preferred_element_type=F32).astype(BF16)


def _value_up(o_lat, wuv_t):
    n = o_lat.shape[0]
    full = lambda shape: pl.BlockSpec(shape, lambda i: (0,) * len(shape))
    return pl.pallas_call(
        _value_up_kernel,
        grid=(1,),
        in_specs=[full((n, MLA_HEADS * KV_RANK)), full((MLA_HEADS, KV_RANK, V_HEAD))],
        out_specs=full((n, MLA_HEADS * V_HEAD)),
        out_shape=jax.ShapeDtypeStruct((n, MLA_HEADS * V_HEAD), BF16),
        compiler_params=_cparams("arbitrary"),
        name="value_up",
    )(o_lat.reshape(n, MLA_HEADS * KV_RANK), wuv_t)


def _layer_norm(v, g, b):
    mu = jnp.mean(v, axis=-1, keepdims=True)
    var = jnp.mean(jnp.square(v - mu), axis=-1, keepdims=True)
    return (v - mu) * lax.rsqrt(var + EPS) * g + b


def _gate_prompt_kernel(u_ref, v_ref, g_ref, b_ref, ws_ref, bias_ref, o_ref):
    tm = u_ref.shape[0]
    vb = _layer_norm(v_ref[...], g_ref[...], b_ref[...]).astype(BF16)
    row = lax.broadcasted_iota(jnp.int32, (CHUNK, CHUNK), 0)
    col = lax.broadcasted_iota(jnp.int32, (CHUNK, CHUNK), 1)
    causal = col <= row
    for g in range(GMLP_GROUPS):
        wc = jnp.where(causal, ws_ref[g], 0.0).astype(BF16)
        cs = slice(g * GMLP_GROUP_DIM, (g + 1) * GMLP_GROUP_DIM)
        for c in range(tm // CHUNK):
            rs = slice(c * CHUNK, (c + 1) * CHUNK)
            sg = jnp.dot(wc, vb[rs, cs], preferred_element_type=F32) + bias_ref[:, cs]
            o_ref[rs, cs] = (u_ref[rs, cs] * sg).astype(BF16)


def _gate_prompt(z, ln_g, ln_b, w_s, bias, *, tm):
    t = z.shape[0]
    d = GMLP_HALF
    full = lambda shape: pl.BlockSpec(shape, lambda i: (0,) * len(shape))
    return pl.pallas_call(
        _gate_prompt_kernel,
        grid=(t // tm,),
        in_specs=[
            pl.BlockSpec((tm, d), lambda i: (i, 0)),
            pl.BlockSpec((tm, d), lambda i: (i, 1)),
            full((1, d)), full((1, d)), full((GMLP_GROUPS, CHUNK, CHUNK)), full((CHUNK, d)),
        ],
        out_specs=pl.BlockSpec((tm, d), lambda i: (i, 0)),
        out_shape=jax.ShapeDtypeStruct((t, d), BF16),
        compiler_params=_cparams("parallel"),
        name="gate_prompt",
    )(z, z, ln_g.reshape(1, d), ln_b.reshape(1, d), w_s, bias)


def _gate_step_kernel(u_ref, v_ref, g_ref, b_ref, w0_ref, b0_ref, o_ref, vout_ref):
    v = _layer_norm(v_ref[...], g_ref[...], b_ref[...])
    vout_ref[...] = v
    sg = w0_ref[...].astype(BF16).astype(F32) * v.astype(BF16).astype(F32) + b0_ref[...]
    o_ref[...] = (u_ref[...] * sg).astype(BF16)


def _gate_step(z, ln_g, ln_b, w0, b0):
    n = z.shape[0]
    d = GMLP_HALF
    full = lambda shape: pl.BlockSpec(shape, lambda i: (0,) * len(shape))
    return pl.pallas_call(
        _gate_step_kernel,
        grid=(1,),
        in_specs=[
            pl.BlockSpec((n, d), lambda i: (0, 0)),
            pl.BlockSpec((n, d), lambda i: (0, 1)),
            full((1, d)), full((1, d)), full((1, d)), full((1, d)),
        ],
        out_specs=[full((n, d)), full((n, d))],
        out_shape=[jax.ShapeDtypeStruct((n, d), BF16), jax.ShapeDtypeStruct((n, d), F32)],
        compiler_params=_cparams("arbitrary"),
        name="gate_step",
    )(z, z, ln_g.reshape(1, d), ln_b.reshape(1, d), w0.reshape(1, d), b0.reshape(1, d))


def _matmul_res_kernel(a_ref, w_ref, r_ref, o_ref):
    o_ref[...] = r_ref[...] + jnp.dot(a_ref[...], w_ref[...], preferred_element_type=F32)


def _matmul_res(a, w, res, *, tm, tn):
    t, k = a.shape
    n = w.shape[1]
    return pl.pallas_call(
        _matmul_res_kernel,
        grid=(t // tm, n // tn),
        in_specs=[
            pl.BlockSpec((tm, k), lambda i, j: (i, 0)),
            pl.BlockSpec((k, tn), lambda i, j: (0, j)),
            pl.BlockSpec((tm, tn), lambda i, j: (i, j)),
        ],
        out_specs=pl.BlockSpec((tm, tn), lambda i, j: (i, j)),
        out_shape=jax.ShapeDtypeStruct((t, n), F32),
        compiler_params=_cparams("parallel", "arbitrary"),
        name="matmul_res",
    )(a, w, res)


def _pack_w_in(w_in):
    q_dim = MLA_HEADS * (QK_NOPE + QK_ROPE)
    q = w_in[:, :q_dim].reshape(D_MODEL, MLA_HEADS, QK_NOPE + QK_ROPE)
    qn = q[:, :, :QK_NOPE].reshape(D_MODEL, MLA_HEADS * QK_NOPE)
    qp = q[:, :, QK_NOPE:].reshape(D_MODEL, MLA_HEADS * QK_ROPE)
    ckv = w_in[:, q_dim:q_dim + KV_RANK]
    kpe = w_in[:, q_dim + KV_RANK:q_dim + KV_RANK + QK_ROPE]
    xb = w_in[:, q_dim + KV_RANK + QK_ROPE:q_dim + KV_RANK + QK_ROPE + LRU_WIDTH]
    gb = w_in[:, q_dim + KV_RANK + QK_ROPE + LRU_WIDTH:]
    pad = jnp.zeros((D_MODEL, LANES - QK_ROPE), w_in.dtype)
    packed = jnp.concatenate([qn, xb, gb, qp, ckv, kpe, pad], axis=1)
    assert packed.shape[1] == PROJ_WIDTH
    return packed.astype(BF16)


def _rope_tables(pos):
    half = QK_ROPE // 2
    inv = ROPE_THETA ** (-jnp.arange(half, dtype=F32) / half)
    ang = pos.astype(F32)[:, None] * inv[None, :]
    cos = jnp.cos(ang)
    sin = jnp.sin(ang)
    reps = LANES // QK_ROPE
    cos_tab = jnp.tile(jnp.concatenate([cos, cos], axis=1), (1, reps))
    sin_tab = jnp.tile(jnp.concatenate([-sin, sin], axis=1), (1, reps))
    return cos_tab, sin_tab


class _Plan(NamedTuple):
    tm_prompt: int
    tm_sample: int
    t_small: int
    tq: int
    tk: int
    page_group: int
    tf: int


def _plan(batch, seq, n_dec, n_pages):
    t = batch * seq
    tm_prompt = 1024 if t % 1024 == 0 else t
    t_small = 512 if seq % 512 == 0 else seq
    tk = 512 if seq % 512 == 0 else seq
    return _Plan(tm_prompt=tm_prompt, tm_sample=n_dec, t_small=t_small, tq=min(256, tk), tk=tk,
                 page_group=8 if n_pages % 8 == 0 else 1, tf=512)


def kernel(x_prompt, x_sample, cache_latent, cache_kpe, state_lru_h, state_conv, page_table,
           norm_mix, norm_ffn, norm_final, ab_w_in, ab_g_kv, ab_w_uk, ab_w_uv, ab_conv_w, ab_conv_b,
           ab_w_a, ab_b_a, ab_w_x, ab_b_x, ab_lambda, ab_w_out, c_w_in, c_ln_g, c_ln_b, c_w_s, c_b_s,
           c_w_out, ffn_w1, ffn_w2):
    batch, seq, _ = x_prompt.shape
    n_dec, dec_seq, _ = x_sample.shape
    assert dec_seq == 1
    n_pages = page_table.shape[1]
    past_len = n_pages * PAGE_SIZE
    depth = norm_mix.shape[0]

    cos_p, sin_p = _rope_tables(jnp.arange(seq, dtype=jnp.int32))
    cos_s, sin_s = _rope_tables(jnp.full((n_dec,), past_len, jnp.int32))
    cache_kpe_t = jnp.swapaxes(cache_kpe, 2, 3)
    w1_all = ffn_w1.astype(BF16)
    w2_all = ffn_w2.astype(BF16)

    hp = x_prompt.reshape(batch * seq, D_MODEL)
    hs = x_sample.reshape(n_dec, D_MODEL)
    plan = _plan(batch, seq, n_dec, n_pages)
    tp = plan.tm_prompt
    tsm = plan.tm_sample

    lat_p, kpe_p, lru_p, conv_p = [], [], [], []
    lat_s, kpe_s, lru_s, conv_s, v_s = [], [], [], [], []

    for l in range(depth):
        j = l // 2
        if l % 2 == 0:
            w_in = _pack_w_in(ab_w_in[j])
            wuk_t = jnp.transpose(ab_w_uk[j], (1, 2, 0)).astype(BF16)
            wuv_t = jnp.transpose(ab_w_uv[j], (1, 0, 2)).astype(BF16)
            w_a = ab_w_a[j].astype(BF16)
            w_x = ab_w_x[j].astype(BF16)
            w_out = ab_w_out[j].astype(BF16)
            lru_args = (ab_conv_w[j], ab_conv_b[j], w_a, ab_b_a[j], w_x, ab_b_x[j], ab_lambda[j])

            proj = _norm_matmul(hp, norm_mix[l], w_in, tm=tp, tn=PROJ_WIDTH // 3)
            q, k, lat, kpe = _prep(proj, cos_p, sin_p, ab_g_kv[j], wuk_t, tm=plan.t_small)
            o_mla = _flash_prompt(q, k, wuv_t, batch=batch, seq=seq, tq=plan.tq, tk=plan.tk)
            o_lru, h_last, conv_new = _lru_prompt(proj, *lru_args, batch=batch, seq=seq, ts=plan.t_small)
            hp = _pair_matmul_res(o_mla, o_lru, w_out, hp, tm=tp, tn=1024)
            lat_p.append(lat.reshape(batch, seq, KV_RANK))
            kpe_p.append(kpe.reshape(batch, seq, QK_ROPE))
            lru_p.append(h_last.reshape(batch, LRU_WIDTH))
            conv_p.append(conv_new)

            proj = _norm_matmul(hs, norm_mix[l], w_in, tm=tsm, tn=PROJ_WIDTH // 3)
            q, k, lat, kpe = _prep(proj, cos_s, sin_s, ab_g_kv[j], wuk_t, tm=tsm)
            o_lat = _paged_attention(page_table, jnp.transpose(q, (1, 0, 2)),
                                     k.reshape(n_dec, 1, QK_WIDTH), cache_latent, cache_kpe_t,
                                     layer=j, group=plan.page_group)
            o_mla = _value_up(o_lat, wuv_t)
            o_lru, h_new, conv_new = _lru_step(
                proj, state_conv[j].reshape(n_dec, (CONV_W - 1) * LRU_WIDTH), state_lru_h[j], *lru_args)
            hs = _pair_matmul_res(o_mla, o_lru, w_out, hs, tm=tsm, tn=1024)
            lat_s.append(lat.reshape(n_dec, 1, KV_RANK))
            kpe_s.append(kpe.reshape(n_dec, 1, QK_ROPE))
            lru_s.append(h_new)
            conv_s.append(conv_new.reshape(n_dec, CONV_W - 1, LRU_WIDTH))
        else:
            w_in = c_w_in[j].astype(BF16)
            w_out = c_w_out[j].astype(BF16)
            bias = jnp.repeat(c_b_s[j].T, GMLP_GROUP_DIM, axis=1)
            w0 = jnp.repeat(c_w_s[j][:, 0, 0], GMLP_GROUP_DIM)

            z = _norm_matmul(hp, norm_mix[l], w_in, tm=tp, tn=1024, act="gelu")
            a = _gate_prompt(z, c_ln_g[j], c_ln_b[j], c_w_s[j], bias, tm=plan.t_small)
            hp = _matmul_res(a, w_out, hp, tm=tp, tn=1024)

            z = _norm_matmul(hs, norm_mix[l], w_in, tm=tsm, tn=1024, act="gelu")
            a, v = _gate_step(z, c_ln_g[j], c_ln_b[j], w0, bias[0])
            hs = _matmul_res(a, w_out, hs, tm=tsm, tn=1024)
            v_s.append(v.reshape(n_dec, 1, GMLP_HALF))

        last = l == depth - 1
        hp = _ffn(hp, norm_ffn[l], w1_all, w2_all, norm_final, layer=l, tm=tp, tf=plan.tf, final_norm=last)
        hs = _ffn(hs, norm_ffn[l], w1_all, w2_all, norm_final, layer=l, tm=tsm, tf=plan.tf, final_norm=last)

    return (hp.reshape(batch, seq, D_MODEL), hs.reshape(n_dec, 1, D_MODEL),
            jnp.stack(lat_p), jnp.stack(kpe_p), jnp.stack(lru_p), jnp.stack(conv_p),
            jnp.stack(lat_s), jnp.stack(kpe_s), jnp.stack(lru_s), jnp.stack(conv_s), jnp.stack(v_s))
```

```python
import functools
from typing import NamedTuple

import numpy as np
import jax
import jax.numpy as jnp
from jax import lax
from jax.experimental import pallas as pl
from jax.experimental.pallas import tpu as pltpu

D_MODEL = 2048
MLA_HEADS = 8
QK_NOPE = 128
QK_ROPE = 64
V_HEAD = 128
KV_RANK = 512
ROPE_THETA = 10000.0
ATTN_SCALE = (QK_NOPE + QK_ROPE) ** -0.5
LRU_WIDTH = D_MODEL // 2
LRU_HEADS = 8
LRU_HEAD_DIM = LRU_WIDTH // LRU_HEADS
CONV_W = 4
LRU_C = 8.0
GMLP_HALF = D_MODEL
CHUNK = 128
GMLP_GROUPS = 8
GMLP_GROUP_DIM = GMLP_HALF // GMLP_GROUPS
D_FF = 4 * D_MODEL
EPS = 1e-6
NEG_INF = -1e30
PAGE_SIZE = 128

LANES = 128
V7X_VMEM_BYTES = 64 * 1024 * 1024
VMEM_LIMIT = (V7X_VMEM_BYTES * 7) // 8

FUSED_PAGE_BUFFERS = 8

QK_WIDTH = KV_RANK + LANES

COL_QN = 0
COL_XB = 1024
COL_GB = 2048
COL_QP = 3072
COL_CKV = 3584
COL_KPE = 4096
PROJ_WIDTH = 4224

BF16 = jnp.bfloat16
F32 = jnp.float32


def _cparams(*sem):
    return pltpu.CompilerParams(dimension_semantics=sem, vmem_limit_bytes=VMEM_LIMIT)


def _lane_repeat(x, n):
    return jnp.concatenate([x] * n, axis=1)


def _rms(x, g):
    return x * lax.rsqrt(jnp.mean(x * x, axis=-1, keepdims=True) + EPS) * g


def _norm_matmul_kernel(x_ref, g_ref, w_ref, o_ref, xn_ref, *, act):
    @pl.when(pl.program_id(1) == 0)
    def _():
        xn_ref[...] = _rms(x_ref[...], g_ref[...]).astype(BF16)

    y = jnp.dot(xn_ref[...], w_ref[...], preferred_element_type=F32)
    if act == "gelu":
        y = jax.nn.gelu(y)
    o_ref[...] = y


def _norm_matmul(x, g, w, *, tm, tn, act=None):
    t, k = x.shape
    n = w.shape[1]
    return pl.pallas_call(
        functools.partial(_norm_matmul_kernel, act=act),
        grid=(t // tm, n // tn),
        in_specs=[
            pl.BlockSpec((tm, k), lambda i, j: (i, 0)),
            pl.BlockSpec((1, k), lambda i, j: (0, 0)),
            pl.BlockSpec((k, tn), lambda i, j: (0, j)),
        ],
        out_specs=pl.BlockSpec((tm, tn), lambda i, j: (i, j)),
        out_shape=jax.ShapeDtypeStruct((t, n), F32),
        scratch_shapes=[pltpu.VMEM((tm, k), BF16)],
        compiler_params=_cparams("parallel", "arbitrary"),
        name="norm_matmul",
    )(x, g.reshape(1, k), w)


def _pair_matmul_res_kernel(a_ref, b_ref, wa_ref, wb_ref, r_ref, o_ref):
    y = jnp.dot(a_ref[...], wa_ref[...], preferred_element_type=F32)
    y = y + jnp.dot(b_ref[...], wb_ref[...], preferred_element_type=F32)
    o_ref[...] = r_ref[...] + y


def _pair_matmul_res(a, b, w, res, *, tm, tn):
    t, ka = a.shape
    kb = b.shape[1]
    assert ka == kb and w.shape[0] == ka + kb
    n = w.shape[1]
    return pl.pallas_call(
        _pair_matmul_res_kernel,
        grid=(t // tm, n // tn),
        in_specs=[
            pl.BlockSpec((tm, ka), lambda i, j: (i, 0)),
            pl.BlockSpec((tm, kb), lambda i, j: (i, 0)),
            pl.BlockSpec((ka, tn), lambda i, j: (0, j)),
            pl.BlockSpec((kb, tn), lambda i, j: (1, j)),
            pl.BlockSpec((tm, tn), lambda i, j: (i, j)),
        ],
        out_specs=pl.BlockSpec((tm, tn), lambda i, j: (i, j)),
        out_shape=jax.ShapeDtypeStruct((t, n), F32),
        compiler_params=_cparams("parallel", "arbitrary"),
        name="pair_matmul_res",
    )(a, b, w, w, res)


def _ffn_step(x_ref, g_ref, w1_ref, w2_ref, gf_ref, o_ref, xn_ref, *, final_norm, col_chunk):
    j = pl.program_id(1)

    @pl.when(j == 0)
    def _():
        x = x_ref[...]
        xn_ref[...] = _rms(x, g_ref[...]).astype(BF16)
        o_ref[...] = x

    a = jnp.dot(xn_ref[...], w1_ref[...], preferred_element_type=F32)
    a = jnp.square(jnp.maximum(a, 0.0)).astype(BF16)
    for c in range(o_ref.shape[1] // col_chunk):
        sl = slice(c * col_chunk, (c + 1) * col_chunk)
        o_ref[:, sl] += jnp.dot(a, w2_ref[:, sl], preferred_element_type=F32)

    if final_norm:
        @pl.when(j == pl.num_programs(1) - 1)
        def _():
            o_ref[...] = _rms(o_ref[...], gf_ref[...])


def _ffn_kernel(x_ref, g_ref, w1_ref, w2_ref, gf_ref, o_ref, xn_ref, *, final_norm, col_chunk):
    _ffn_step(x_ref, g_ref, w1_ref, w2_ref, gf_ref, o_ref, xn_ref, final_norm=final_norm,
              col_chunk=col_chunk)


def _ffn(x, g, w1, w2, g_final, *, layer, tm, tf, final_norm):
    t, d = x.shape
    f = w1.shape[2]
    return pl.pallas_call(
        functools.partial(_ffn_kernel, final_norm=final_norm, col_chunk=512),
        grid=(t // tm, f // tf),
        in_specs=[
            pl.BlockSpec((tm, d), lambda i, j: (i, 0)),
            pl.BlockSpec((1, d), lambda i, j: (0, 0)),
            pl.BlockSpec((None, d, tf), lambda i, j: (layer, 0, j)),
            pl.BlockSpec((None, tf, d), lambda i, j: (layer, j, 0)),
            pl.BlockSpec((1, d), lambda i, j: (0, 0)),
        ],
        out_specs=pl.BlockSpec((tm, d), lambda i, j: (i, 0)),
        out_shape=jax.ShapeDtypeStruct((t, d), F32),
        scratch_shapes=[pltpu.VMEM((tm, d), BF16)],
        compiler_params=_cparams("parallel", "arbitrary"),
        name="ffn",
    )(x, g.reshape(1, d), w1, w2, g_final.reshape(1, d))


def _prep_kernel(qn_ref, qp_ref, ckv_ref, kpe_ref, cos_ref, sin_ref, gkv_ref, wuk_ref,
                 q_ref, k_ref, lat_ref, kpeo_ref):
    tm = qn_ref.shape[0]
    cos = cos_ref[...]
    sin = sin_ref[...]
    lane = lax.broadcasted_iota(jnp.int32, (tm, LANES), 1)
    first_half = (lane & (QK_ROPE - 1)) < (QK_ROPE // 2)
    low = lane < QK_ROPE

    def rope(x):
        rot = jnp.where(first_half,
                        pltpu.roll(x, LANES - QK_ROPE // 2, 1),
                        pltpu.roll(x, QK_ROPE // 2, 1))
        return x * cos + rot * sin

    for pair in range(MLA_HEADS // 2):
        xp = rope(qp_ref[:, pair * LANES:(pair + 1) * LANES]) * ATTN_SCALE
        even = jnp.where(low, xp, 0.0)
        odd = jnp.where(low, pltpu.roll(xp, QK_ROPE, 1), 0.0)
        q_ref[2 * pair, :, KV_RANK:QK_WIDTH] = even.astype(BF16)
        q_ref[2 * pair + 1, :, KV_RANK:QK_WIDTH] = odd.astype(BF16)

    for h in range(MLA_HEADS):
        qn = qn_ref[:, h * QK_NOPE:(h + 1) * QK_NOPE].astype(BF16)
        ql = jnp.dot(qn, wuk_ref[h], preferred_element_type=F32) * ATTN_SCALE
        q_ref[h, :, 0:KV_RANK] = ql.astype(BF16)

    ckv = _rms(ckv_ref[...], gkv_ref[...])
    lat_ref[...] = ckv
    k_ref[:, 0:KV_RANK] = ckv.astype(BF16)
    kp = rope(kpe_ref[...])
    kpeo_ref[...] = kp[:, 0:QK_ROPE]
    k_ref[:, KV_RANK:QK_WIDTH] = kp.astype(BF16)


def _prep(proj, cos_tab, sin_tab, g_kv, wuk_t, *, tm):
    t = proj.shape[0]
    n_tab = cos_tab.shape[0] // tm

    def col(width, start):
        return pl.BlockSpec((tm, width), lambda i: (i, start // width))

    return pl.pallas_call(
        _prep_kernel,
        grid=(t // tm,),
        in_specs=[
            col(MLA_HEADS * QK_NOPE, COL_QN),
            col(MLA_HEADS * QK_ROPE, COL_QP),
            col(KV_RANK, COL_CKV),
            col(LANES, COL_KPE),
            pl.BlockSpec((tm, LANES), lambda i: (i % n_tab, 0)),
            pl.BlockSpec((tm, LANES), lambda i: (i % n_tab, 0)),
            pl.BlockSpec((1, KV_RANK), lambda i: (0, 0)),
            pl.BlockSpec((MLA_HEADS, QK_NOPE, KV_RANK), lambda i: (0, 0, 0)),
        ],
        out_specs=[
            pl.BlockSpec((MLA_HEADS, tm, QK_WIDTH), lambda i: (0, i, 0)),
            pl.BlockSpec((tm, QK_WIDTH), lambda i: (i, 0)),
            pl.BlockSpec((tm, KV_RANK), lambda i: (i, 0)),
            pl.BlockSpec((tm, QK_ROPE), lambda i: (i, 0)),
        ],
        out_shape=[
            jax.ShapeDtypeStruct((MLA_HEADS, t, QK_WIDTH), BF16),
            jax.ShapeDtypeStruct((t, QK_WIDTH), BF16),
            jax.ShapeDtypeStruct((t, KV_RANK), F32),
            jax.ShapeDtypeStruct((t, QK_ROPE), F32),
        ],
        compiler_params=_cparams("parallel"),
        name="mla_prep",
    )(proj, proj, proj, proj, cos_tab, sin_tab, g_kv.reshape(1, KV_RANK), wuk_t)


def _flash_kernel(qi_ref, ki_ref, q_ref, k_ref, wuv_ref, o_ref, m_ref, l_ref, acc_ref, *, tq, tk):
    p_id = pl.program_id(1)
    qi = qi_ref[p_id]
    ki = ki_ref[p_id]
    rows = MLA_HEADS * tq

    @pl.when(ki == 0)
    def _():
        m_ref[...] = jnp.full(m_ref.shape, NEG_INF, F32)
        l_ref[...] = jnp.zeros(l_ref.shape, F32)
        acc_ref[...] = jnp.zeros(acc_ref.shape, F32)

    q = q_ref[...].reshape(rows, QK_WIDTH)
    k = k_ref[...]
    s = lax.dot_general(q, k, (((1,), (1,)), ((), ())), preferred_element_type=F32)
    q_pos = qi * tq + (lax.broadcasted_iota(jnp.int32, (rows, tk), 0) & (tq - 1))
    k_pos = ki * tk + lax.broadcasted_iota(jnp.int32, (rows, tk), 1)
    s = jnp.where(k_pos <= q_pos, s, NEG_INF)

    m_prev = m_ref[...]
    m_next = jnp.maximum(m_prev, jnp.max(s, axis=1, keepdims=True))
    alpha = jnp.exp(m_prev - m_next)
    p = jnp.exp(s - _lane_repeat(m_next, tk // LANES))
    l_ref[...] = alpha * l_ref[...] + jnp.sum(p, axis=1, keepdims=True)
    m_ref[...] = m_next
    pv = jnp.dot(p.astype(BF16), k[:, 0:KV_RANK], preferred_element_type=F32)
    acc_ref[...] = acc_ref[...] * _lane_repeat(alpha, KV_RANK // LANES) + pv

    @pl.when((ki + 1) * tk >= (qi + 1) * tq)
    def _():
        inv = 1.0 / l_ref[...]
        o = (acc_ref[...] * _lane_repeat(inv, KV_RANK // LANES)).astype(BF16)
        o = o.reshape(MLA_HEADS, tq, KV_RANK)
        for h in range(MLA_HEADS):
            o_ref[:, h * V_HEAD:(h + 1) * V_HEAD] = jnp.dot(
                o[h], wuv_ref[h], preferred_element_type=F32).astype(BF16)


def _flash_prompt(q, k, wuv_t, *, batch, seq, tq, tk):
    assert tk % tq == 0 and seq % tk == 0
    nq = seq // tq
    qi_list, ki_list = [], []
    for qi in range(nq):
        for ki in range(-(-((qi + 1) * tq) // tk)):
            qi_list.append(qi)
            ki_list.append(ki)
    n_pairs = len(qi_list)
    qi_tab = jnp.asarray(np.asarray(qi_list, np.int32))
    ki_tab = jnp.asarray(np.asarray(ki_list, np.int32))
    nkb = seq // tk
    grid_spec = pltpu.PrefetchScalarGridSpec(
        num_scalar_prefetch=2,
        grid=(batch, n_pairs),
        in_specs=[
            pl.BlockSpec((MLA_HEADS, tq, QK_WIDTH), lambda b, p, qt, kt: (0, b * nq + qt[p], 0)),
            pl.BlockSpec((tk, QK_WIDTH), lambda b, p, qt, kt: (b * nkb + kt[p], 0)),
            pl.BlockSpec((MLA_HEADS, KV_RANK, V_HEAD), lambda b, p, qt, kt: (0, 0, 0)),
        ],
        out_specs=pl.BlockSpec((tq, MLA_HEADS * V_HEAD), lambda b, p, qt, kt: (b * nq + qt[p], 0)),
        scratch_shapes=[
            pltpu.VMEM((MLA_HEADS * tq, LANES), F32),
            pltpu.VMEM((MLA_HEADS * tq, LANES), F32),
            pltpu.VMEM((MLA_HEADS * tq, KV_RANK), F32),
        ],
    )
    return pl.pallas_call(
        functools.partial(_flash_kernel, tq=tq, tk=tk),
        grid_spec=grid_spec,
        out_shape=jax.ShapeDtypeStruct((batch * seq, MLA_HEADS * V_HEAD), BF16),
        compiler_params=_cparams("parallel", "arbitrary"),
        name="flash_prompt",
    )(qi_tab, ki_tab, q, k, wuv_t)


def _expm1(y):
    return jnp.tanh(0.5 * y) * (jnp.exp(y) + 1.0)


def _lru_gates(xc, wa_ref, ba_ref, wx_ref, bx_ref, lam_ref):
    xcb = xc.astype(BF16)
    gr, gi = [], []
    for h in range(LRU_HEADS):
        xh = xcb[:, h * LRU_HEAD_DIM:(h + 1) * LRU_HEAD_DIM]
        gr.append(jnp.dot(xh, wa_ref[h], preferred_element_type=F32))
        gi.append(jnp.dot(xh, wx_ref[h], preferred_element_type=F32))
    gate_r = jnp.concatenate(gr, axis=1) + ba_ref[...]
    gate_i = jnp.concatenate(gi, axis=1) + bx_ref[...]
    r = jax.nn.sigmoid(gate_r)
    i = jax.nn.sigmoid(gate_i)
    log_a = -LRU_C * r * jax.nn.softplus(-lam_ref[...])
    a = jnp.exp(log_a)
    u = jnp.sqrt(-_expm1(2.0 * log_a)) * i * xc
    return a, u


def _lru_prompt_kernel(xb_ref, gb_ref, cw_ref, cb_ref, wa_ref, ba_ref, wx_ref, bx_ref, lam_ref,
                       o_ref, hlast_ref, conv_ref,
                       xpad_ref, a_ref, u_ref, hs_ref, h_ref):
    ts = xb_ref.shape[0]
    t_id = pl.program_id(1)
    halo = 8

    @pl.when(t_id == 0)
    def _():
        xpad_ref[0:halo, :] = jnp.zeros((halo, LRU_WIDTH), F32)
        h_ref[...] = jnp.zeros(h_ref.shape, F32)

    @pl.when(t_id > 0)
    def _():
        xpad_ref[0:halo, :] = xpad_ref[ts:ts + halo, :]

    xpad_ref[halo:halo + ts, :] = xb_ref[...]
    xc = cb_ref[...]
    for kk in range(CONV_W):
        off = halo - (CONV_W - 1) + kk
        xc = xc + xpad_ref[off:off + ts, :] * cw_ref[kk:kk + 1, :]
    conv_ref[0] = xpad_ref[ts + halo - (CONV_W - 1):ts + halo, :]

    a, u = _lru_gates(xc, wa_ref, ba_ref, wx_ref, bx_ref, lam_ref)
    a_ref[...] = a
    u_ref[...] = u

    def step(t, h):
        h = a_ref[pl.ds(t, 1), :] * h + u_ref[pl.ds(t, 1), :]
        hs_ref[pl.ds(t, 1), :] = h
        return h

    h = lax.fori_loop(0, ts, step, h_ref[0:1, :], unroll=8)
    h_ref[0:1, :] = h
    hlast_ref[0] = h
    o_ref[...] = (hs_ref[...] * jax.nn.gelu(gb_ref[...])).astype(BF16)


def _lru_prompt(proj, conv_w, conv_b, w_a, b_a, w_x, b_x, lam, *, batch, seq, ts):
    nt = seq // ts
    w = LRU_WIDTH
    vec = lambda: pl.BlockSpec((1, w), lambda b, t: (0, 0))
    hw = lambda: pl.BlockSpec((LRU_HEADS, LRU_HEAD_DIM, LRU_HEAD_DIM), lambda b, t: (0, 0, 0))
    return pl.pallas_call(
        _lru_prompt_kernel,
        grid=(batch, nt),
        in_specs=[
            pl.BlockSpec((ts, w), lambda b, t: (b * nt + t, COL_XB // w)),
            pl.BlockSpec((ts, w), lambda b, t: (b * nt + t, COL_GB // w)),
            pl.BlockSpec((CONV_W, w), lambda b, t: (0, 0)),
            vec(), hw(), vec(), hw(), vec(), vec(),
        ],
        out_specs=[
            pl.BlockSpec((ts, w), lambda b, t: (b * nt + t, 0)),
            pl.BlockSpec((1, 1, w), lambda b, t: (b, 0, 0)),
            pl.BlockSpec((1, CONV_W - 1, w), lambda b, t: (b, 0, 0)),
        ],
        out_shape=[
            jax.ShapeDtypeStruct((batch * seq, w), BF16),
            jax.ShapeDtypeStruct((batch, 1, w), F32),
            jax.ShapeDtypeStruct((batch, CONV_W - 1, w), F32),
        ],
        scratch_shapes=[
            pltpu.VMEM((ts + 8, w), F32),
            pltpu.VMEM((ts, w), F32),
            pltpu.VMEM((ts, w), F32),
            pltpu.VMEM((ts, w), F32),
            pltpu.VMEM((8, w), F32),
        ],
        compiler_params=_cparams("parallel", "arbitrary"),
        name="lru_prompt",
    )(proj, proj, conv_w, conv_b.reshape(1, w), w_a, b_a.reshape(1, w), w_x, b_x.reshape(1, w),
      lam.reshape(1, w))


def _lru_step_kernel(xb_ref, gb_ref, buf_ref, h0_ref, cw_ref, cb_ref, wa_ref, ba_ref, wx_ref, bx_ref,
                     lam_ref, o_ref, hnew_ref, conv_ref):
    w = LRU_WIDTH
    xb = xb_ref[...]
    xc = cb_ref[...]
    for kk in range(CONV_W - 1):
        xc = xc + buf_ref[:, kk * w:(kk + 1) * w] * cw_ref[kk:kk + 1, :]
    xc = xc + xb * cw_ref[CONV_W - 1:CONV_W, :]
    for kk in range(1, CONV_W - 1):
        conv_ref[:, (kk - 1) * w:kk * w] = buf_ref[:, kk * w:(kk + 1) * w]
    conv_ref[:, (CONV_W - 2) * w:(CONV_W - 1) * w] = xb
    a, u = _lru_gates(xc, wa_ref, ba_ref, wx_ref, bx_ref, lam_ref)
    h = a * h0_ref[...] + u
    hnew_ref[...] = h
    o_ref[...] = (h * jax.nn.gelu(gb_ref[...])).astype(BF16)


def _lru_step(proj, conv_buf, h0, conv_w, conv_b, w_a, b_a, w_x, b_x, lam):
    n = proj.shape[0]
    w = LRU_WIDTH
    full = lambda shape: pl.BlockSpec(shape, lambda i: (0,) * len(shape))
    return pl.pallas_call(
        _lru_step_kernel,
        grid=(1,),
        in_specs=[
            pl.BlockSpec((n, w), lambda i: (0, COL_XB // w)),
            pl.BlockSpec((n, w), lambda i: (0, COL_GB // w)),
            full((n, (CONV_W - 1) * w)), full((n, w)), full((CONV_W, w)), full((1, w)),
            full((LRU_HEADS, LRU_HEAD_DIM, LRU_HEAD_DIM)), full((1, w)),
            full((LRU_HEADS, LRU_HEAD_DIM, LRU_HEAD_DIM)), full((1, w)), full((1, w)),
        ],
        out_specs=[full((n, w)), full((n, w)), full((n, (CONV_W - 1) * w))],
        out_shape=[
            jax.ShapeDtypeStruct((n, w), BF16),
            jax.ShapeDtypeStruct((n, w), F32),
            jax.ShapeDtypeStruct((n, (CONV_W - 1) * w), F32),
        ],
        compiler_params=_cparams("arbitrary"),
        name="lru_step",
    )(proj, proj, conv_buf, h0, conv_w, conv_b.reshape(1, w), w_a, b_a.reshape(1, w), w_x,
      b_x.reshape(1, w), lam.reshape(1, w))


def _paged_share(step, pt_ref, q_ref, kn_ref, lat_hbm, kpe_hbm, o_ref,
                 latbuf, kpebuf, sem, latb_ref, s_ref, m_ref, l_ref, acc_ref,
                 *, layer, group, n_groups, sample0, n_samples, step_groups):
    total = n_samples * n_groups
    n_buf = latbuf.shape[0]
    g_base = step * step_groups

    def copies(g, slot):
        gc = jnp.minimum(g, total - 1)
        sb = sample0 + gc // n_groups
        sj = gc % n_groups
        out = []
        for i in range(group):
            page = pt_ref[sb, sj * group + i]
            keys = pl.ds(i * PAGE_SIZE, PAGE_SIZE)
            out.append(pltpu.make_async_copy(lat_hbm.at[layer, page], latbuf.at[slot, keys], sem.at[0, slot]))
            out.append(pltpu.make_async_copy(kpe_hbm.at[layer, page], kpebuf.at[slot, :, keys], sem.at[1, slot]))
        return out

    def wait_group(g):
        for c in copies(g, g % n_buf):
            c.wait()

    def stage_scores(g, parity):
        slot = g % n_buf
        q = q_ref[sample0 + jnp.minimum(g, total - 1) // n_groups]
        lat = latbuf[slot].astype(BF16)
        latb_ref[parity] = lat
        kpe = kpebuf[slot].astype(BF16)
        s = lax.dot_general(q[:, 0:KV_RANK], lat, (((1,), (1,)), ((), ())), preferred_element_type=F32)
        s = s + jnp.dot(q[:, KV_RANK:KV_RANK + QK_ROPE], kpe, preferred_element_type=F32)
        s_ref[parity] = s

    def softmax_step(s):
        m_prev = m_ref[...]
        m_next = jnp.maximum(m_prev, jnp.max(s, axis=1, keepdims=True))
        alpha = jnp.exp(m_prev - m_next)
        p = jnp.exp(s - m_next[:, 0:1])
        l_ref[...] = alpha * l_ref[...] + jnp.sum(p, axis=1, keepdims=True)
        m_ref[...] = m_next
        return p, alpha[:, 0:1]

    @pl.when(step == 0)
    def _():
        for g0 in range(n_buf):
            for c in copies(g0, g0):
                c.start()
        wait_group(0)
        stage_scores(0, 0)

    @pl.when(g_base % n_groups == 0)
    def _():
        m_ref[...] = jnp.full(m_ref.shape, NEG_INF, F32)
        l_ref[...] = jnp.zeros(l_ref.shape, F32)
        acc_ref[...] = jnp.zeros(acc_ref.shape, F32)

    def body(jj, carry):
        for parity in range(2):
            g = g_base + 2 * jj + parity
            wait_group(g + 1)

            @pl.when(g + n_buf <= total)
            def _():
                for c in copies(g + n_buf, g % n_buf):
                    c.start()

            p, alpha = softmax_step(s_ref[parity])
            stage_scores(g + 1, 1 - parity)
            pv = jnp.dot(p.astype(BF16), latb_ref[parity], preferred_element_type=F32)
            acc_ref[...] = acc_ref[...] * alpha + pv
        return carry

    lax.fori_loop(0, step_groups // 2, body, 0)

    @pl.when((g_base + step_groups) % n_groups == 0)
    def _():
        sample = sample0 + g_base // n_groups
        q = q_ref[sample]
        kn = kn_ref[pl.ds(sample, 1), :]
        s_new = jnp.sum(q.astype(F32) * kn, axis=1, keepdims=True)
        p, alpha = softmax_step(s_new)
        acc = acc_ref[...] * alpha + p.astype(BF16).astype(F32) * kn[:, 0:KV_RANK]
        o_ref[0] = acc / l_ref[:, 0:1]


def _ffn_paged_kernel(pt_ref, x_ref, g_ref, w1_ref, w2_ref, gf_ref, q_ref, kn_ref, lat_hbm, kpe_hbm,
                      o_ref, att_ref, xn_ref, latbuf, kpebuf, sem, latb_ref, s_ref, m_ref, l_ref, acc_ref,
                      *, final_norm, col_chunk, **paged):
    _ffn_step(x_ref, g_ref, w1_ref, w2_ref, gf_ref, o_ref, xn_ref, final_norm=final_norm,
              col_chunk=col_chunk)
    step = pl.program_id(0) * pl.num_programs(1) + pl.program_id(1)
    _paged_share(step, pt_ref, q_ref, kn_ref, lat_hbm, kpe_hbm, att_ref,
                 latbuf, kpebuf, sem, latb_ref, s_ref, m_ref, l_ref, acc_ref, **paged)


def _ffn_paged(x, g, w1, w2, g_final, page_table, q, k_new, cache_latent, cache_kpe_t, *,
               mlp_layer, tm, tf, final_norm, cache_layer, group, sample0, n_samples):
    t, d = x.shape
    f = w1.shape[2]
    ni, nj = t // tm, f // tf
    n_pages = page_table.shape[1]
    assert n_pages % group == 0
    n_groups = n_pages // group
    total = n_samples * n_groups
    assert total % (ni * nj) == 0
    step_groups = total // (ni * nj)
    assert step_groups % 2 == 0 and n_groups % step_groups == 0 and total >= FUSED_PAGE_BUFFERS
    keys = group * PAGE_SIZE
    n_dec = q.shape[0]
    once = pl.Buffered(1)

    def att_index(i, j, pt):
        return ((i * nj + j) * step_groups // n_groups, 0, 0)

    grid_spec = pltpu.PrefetchScalarGridSpec(
        num_scalar_prefetch=1,
        grid=(ni, nj),
        in_specs=[
            pl.BlockSpec((tm, d), lambda i, j, pt: (i, 0), pipeline_mode=once),
            pl.BlockSpec((1, d), lambda i, j, pt: (0, 0)),
            pl.BlockSpec((None, d, tf), lambda i, j, pt: (mlp_layer, 0, j)),
            pl.BlockSpec((None, tf, d), lambda i, j, pt: (mlp_layer, j, 0)),
            pl.BlockSpec((1, d), lambda i, j, pt: (0, 0)),
            pl.BlockSpec((n_dec, MLA_HEADS, QK_WIDTH), lambda i, j, pt: (0, 0, 0), pipeline_mode=once),
            pl.BlockSpec((n_dec, QK_WIDTH), lambda i, j, pt: (0, 0), pipeline_mode=once),
            pl.BlockSpec(memory_space=pl.ANY),
            pl.BlockSpec(memory_space=pl.ANY),
        ],
        out_specs=[
            pl.BlockSpec((tm, d), lambda i, j, pt: (i, 0), pipeline_mode=once),
            pl.BlockSpec((1, MLA_HEADS, KV_RANK), att_index),
        ],
        scratch_shapes=[
            pltpu.VMEM((tm, d), BF16),
            pltpu.VMEM((FUSED_PAGE_BUFFERS, keys, KV_RANK), F32),
            pltpu.VMEM((FUSED_PAGE_BUFFERS, QK_ROPE, keys), F32),
            pltpu.SemaphoreType.DMA((2, FUSED_PAGE_BUFFERS)),
            pltpu.VMEM((2, keys, KV_RANK), BF16),
            pltpu.VMEM((2, MLA_HEADS, keys), F32),
            pltpu.VMEM((MLA_HEADS, LANES), F32),
            pltpu.VMEM((MLA_HEADS, LANES), F32),
            pltpu.VMEM((MLA_HEADS, KV_RANK), F32),
        ],
    )
    return pl.pallas_call(
        functools.partial(_ffn_paged_kernel, final_norm=final_norm, col_chunk=512, layer=cache_layer,
                          group=group, n_groups=n_groups, sample0=sample0, n_samples=n_samples,
                          step_groups=step_groups),
        grid_spec=grid_spec,
        out_shape=[jax.ShapeDtypeStruct((t, d), F32),
                   jax.ShapeDtypeStruct((n_samples, MLA_HEADS, KV_RANK), F32)],
        compiler_params=_cparams("arbitrary", "arbitrary"),
        name="ffn_paged",
    )(page_table, x, g.reshape(1, d), w1, w2, g_final.reshape(1, d), q, k_new, cache_latent, cache_kpe_t)


def _value_up_kernel(o_ref, wuv_ref, y_ref):
    for h in range(MLA_HEADS):
        oh = o_ref[:, h * KV_RANK:(h + 1) * KV_RANK].astype(BF16)
        y_ref[:, h * V_HEAD:(h + 1) * V_HEAD] = jnp.dot(
            oh, wuv_ref[h], preferred_element_type=F32).astype(BF16)


def _value_up(o_lat, wuv_t):
    n = o_lat.shape[0]
    full = lambda shape: pl.BlockSpec(shape, lambda i: (0,) * len(shape))
    return pl.pallas_call(
        _value_up_kernel,
        grid=(1,),
        in_specs=[full((n, MLA_HEADS * KV_RANK)), full((MLA_HEADS, KV_RANK, V_HEAD))],
        out_specs=full((n, MLA_HEADS * V_HEAD)),
        out_shape=jax.ShapeDtypeStruct((n, MLA_HEADS * V_HEAD), BF16),
        compiler_params=_cparams("arbitrary"),
        name="value_up",
    )(o_lat.reshape(n, MLA_HEADS * KV_RANK), wuv_t)


def _layer_norm(v, g, b):
    mu = jnp.mean(v, axis=-1, keepdims=True)
    var = jnp.mean(jnp.square(v - mu), axis=-1, keepdims=True)
    return (v - mu) * lax.rsqrt(var + EPS) * g + b


def _gate_prompt_kernel(u_ref, v_ref, g_ref, b_ref, ws_ref, bias_ref, o_ref):
    tm = u_ref.shape[0]
    vb = _layer_norm(v_ref[...], g_ref[...], b_ref[...]).astype(BF16)
    row = lax.broadcasted_iota(jnp.int32, (CHUNK, CHUNK), 0)
    col = lax.broadcasted_iota(jnp.int32, (CHUNK, CHUNK), 1)
    causal = col <= row
    for g in range(GMLP_GROUPS):
        wc = jnp.where(causal, ws_ref[g], 0.0).astype(BF16)
        cs = slice(g * GMLP_GROUP_DIM, (g + 1) * GMLP_GROUP_DIM)
        for c in range(tm // CHUNK):
            rs = slice(c * CHUNK, (c + 1) * CHUNK)
            sg = jnp.dot(wc, vb[rs, cs], preferred_element_type=F32) + bias_ref[:, cs]
            o_ref[rs, cs] = (u_ref[rs, cs] * sg).astype(BF16)


def _gate_prompt(z, ln_g, ln_b, w_s, bias, *, tm):
    t = z.shape[0]
    d = GMLP_HALF
    full = lambda shape: pl.BlockSpec(shape, lambda i: (0,) * len(shape))
    return pl.pallas_call(
        _gate_prompt_kernel,
        grid=(t // tm,),
        in_specs=[
            pl.BlockSpec((tm, d), lambda i: (i, 0)),
            pl.BlockSpec((tm, d), lambda i: (i, 1)),
            full((1, d)), full((1, d)), full((GMLP_GROUPS, CHUNK, CHUNK)), full((CHUNK, d)),
        ],
        out_specs=pl.BlockSpec((tm, d), lambda i: (i, 0)),
        out_shape=jax.ShapeDtypeStruct((t, d), BF16),
        compiler_params=_cparams("parallel"),
        name="gate_prompt",
    )(z, z, ln_g.reshape(1, d), ln_b.reshape(1, d), w_s, bias)


def _gate_step_kernel(u_ref, v_ref, g_ref, b_ref, w0_ref, b0_ref, o_ref, vout_ref):
    v = _layer_norm(v_ref[...], g_ref[...], b_ref[...])
    vout_ref[...] = v
    sg = w0_ref[...].astype(BF16).astype(F32) * v.astype(BF16).astype(F32) + b0_ref[...]
    o_ref[...] = (u_ref[...] * sg).astype(BF16)


def _gate_step(z, ln_g, ln_b, w0, b0):
    n = z.shape[0]
    d = GMLP_HALF
    full = lambda shape: pl.BlockSpec(shape, lambda i: (0,) * len(shape))
    return pl.pallas_call(
        _gate_step_kernel,
        grid=(1,),
        in_specs=[
            pl.BlockSpec((n, d), lambda i: (0, 0)),
            pl.BlockSpec((n, d), lambda i: (0, 1)),
            full((1, d)), full((1, d)), full((1, d)), full((1, d)),
        ],
        out_specs=[full((n, d)), full((n, d))],
        out_shape=[jax.ShapeDtypeStruct((n, d), BF16), jax.ShapeDtypeStruct((n, d), F32)],
        compiler_params=_cparams("arbitrary"),
        name="gate_step",
    )(z, z, ln_g.reshape(1, d), ln_b.reshape(1, d), w0.reshape(1, d), b0.reshape(1, d))


def _matmul_res_kernel(a_ref, w_ref, r_ref, o_ref):
    o_ref[...] = r_ref[...] + jnp.dot(a_ref[...], w_ref[...], preferred_element_type=F32)


def _matmul_res(a, w, res, *, tm, tn):
    t, k = a.shape
    n = w.shape[1]
    return pl.pallas_call(
        _matmul_res_kernel,
        grid=(t // tm, n // tn),
        in_specs=[
            pl.BlockSpec((tm, k), lambda i, j: (i, 0)),
            pl.BlockSpec((k, tn), lambda i, j: (0, j)),
            pl.BlockSpec((tm, tn), lambda i, j: (i, j)),
        ],
        out_specs=pl.BlockSpec((tm, tn), lambda i, j: (i, j)),
        out_shape=jax.ShapeDtypeStruct((t, n), F32),
        compiler_params=_cparams("parallel", "arbitrary"),
        name="matmul_res",
    )(a, w, res)


def _pack_w_in(w_in):
    q_dim = MLA_HEADS * (QK_NOPE + QK_ROPE)
    q = w_in[:, :q_dim].reshape(D_MODEL, MLA_HEADS, QK_NOPE + QK_ROPE)
    qn = q[:, :, :QK_NOPE].reshape(D_MODEL, MLA_HEADS * QK_NOPE)
    qp = q[:, :, QK_NOPE:].reshape(D_MODEL, MLA_HEADS * QK_ROPE)
    ckv = w_in[:, q_dim:q_dim + KV_RANK]
    kpe = w_in[:, q_dim + KV_RANK:q_dim + KV_RANK + QK_ROPE]
    xb = w_in[:, q_dim + KV_RANK + QK_ROPE:q_dim + KV_RANK + QK_ROPE + LRU_WIDTH]
    gb = w_in[:, q_dim + KV_RANK + QK_ROPE + LRU_WIDTH:]
    pad = jnp.zeros((D_MODEL, LANES - QK_ROPE), w_in.dtype)
    packed = jnp.concatenate([qn, xb, gb, qp, ckv, kpe, pad], axis=1)
    assert packed.shape[1] == PROJ_WIDTH
    return packed.astype(BF16)


def _rope_tables(pos):
    half = QK_ROPE // 2
    inv = ROPE_THETA ** (-jnp.arange(half, dtype=F32) / half)
    ang = pos.astype(F32)[:, None] * inv[None, :]
    cos = jnp.cos(ang)
    sin = jnp.sin(ang)
    reps = LANES // QK_ROPE
    cos_tab = jnp.tile(jnp.concatenate([cos, cos], axis=1), (1, reps))
    sin_tab = jnp.tile(jnp.concatenate([-sin, sin], axis=1), (1, reps))
    return cos_tab, sin_tab


class _Plan(NamedTuple):
    tm_prompt: int
    tm_sample: int
    t_small: int
    tq: int
    tk: int
    page_group: int
    tf: int


def _plan(batch, seq, n_dec, n_pages):
    t = batch * seq
    tm_prompt = 1024 if t % 1024 == 0 else t
    t_small = 512 if seq % 512 == 0 else seq
    tk = 512 if seq % 512 == 0 else seq
    return _Plan(tm_prompt=tm_prompt, tm_sample=n_dec, t_small=t_small, tq=min(256, tk), tk=tk,
                 page_group=8 if n_pages % 8 == 0 else 1, tf=512)


def kernel(x_prompt, x_sample, cache_latent, cache_kpe, state_lru_h, state_conv, page_table,
           norm_mix, norm_ffn, norm_final, ab_w_in, ab_g_kv, ab_w_uk, ab_w_uv, ab_conv_w, ab_conv_b,
           ab_w_a, ab_b_a, ab_w_x, ab_b_x, ab_lambda, ab_w_out, c_w_in, c_ln_g, c_ln_b, c_w_s, c_b_s,
           c_w_out, ffn_w1, ffn_w2):
    batch, seq, _ = x_prompt.shape
    n_dec, dec_seq, _ = x_sample.shape
    assert dec_seq == 1
    n_pages = page_table.shape[1]
    past_len = n_pages * PAGE_SIZE
    depth = norm_mix.shape[0]

    cos_p, sin_p = _rope_tables(jnp.arange(seq, dtype=jnp.int32))
    cos_s, sin_s = _rope_tables(jnp.full((n_dec,), past_len, jnp.int32))
    cache_kpe_t = jnp.swapaxes(cache_kpe, 2, 3)
    w1_all = ffn_w1.astype(BF16)
    w2_all = ffn_w2.astype(BF16)

    hp = x_prompt.reshape(batch * seq, D_MODEL)
    hs = x_sample.reshape(n_dec, D_MODEL)
    plan = _plan(batch, seq, n_dec, n_pages)
    tp = plan.tm_prompt
    tsm = plan.tm_sample

    assert depth == 2
    mix_w = _pack_w_in(ab_w_in[0])
    wuk_t = jnp.transpose(ab_w_uk[0], (1, 2, 0)).astype(BF16)
    wuv_t = jnp.transpose(ab_w_uv[0], (1, 0, 2)).astype(BF16)
    mix_out = ab_w_out[0].astype(BF16)
    lru_args = (ab_conv_w[0], ab_conv_b[0], ab_w_a[0].astype(BF16), ab_b_a[0], ab_w_x[0].astype(BF16),
                ab_b_x[0], ab_lambda[0])
    gm_w_in = c_w_in[0].astype(BF16)
    gm_w_out = c_w_out[0].astype(BF16)
    gm_bias = jnp.repeat(c_b_s[0].T, GMLP_GROUP_DIM, axis=1)
    gm_w0 = jnp.repeat(c_w_s[0][:, 0, 0], GMLP_GROUP_DIM)

    proj_s = _norm_matmul(hs, norm_mix[0], mix_w, tm=tsm, tn=PROJ_WIDTH // 3)
    q_s, k_s, lat_s, kpe_s = _prep(proj_s, cos_s, sin_s, ab_g_kv[0], wuk_t, tm=tsm)
    q_s = jnp.transpose(q_s, (1, 0, 2))
    kn_s = k_s.astype(F32)
    half = n_dec // 2
    paged = dict(tm=tp, tf=plan.tf, cache_layer=0, group=plan.page_group)

    proj = _norm_matmul(hp, norm_mix[0], mix_w, tm=tp, tn=PROJ_WIDTH // 3)
    q, k, lat_p, kpe_p = _prep(proj, cos_p, sin_p, ab_g_kv[0], wuk_t, tm=plan.t_small)
    o_mla = _flash_prompt(q, k, wuv_t, batch=batch, seq=seq, tq=plan.tq, tk=plan.tk)
    o_lru, lru_p, conv_p = _lru_prompt(proj, *lru_args, batch=batch, seq=seq, ts=plan.t_small)
    hp = _pair_matmul_res(o_mla, o_lru, mix_out, hp, tm=tp, tn=1024)
    hp, att_a = _ffn_paged(hp, norm_ffn[0], w1_all, w2_all, norm_final, page_table, q_s, kn_s,
                           cache_latent, cache_kpe_t, mlp_layer=0, final_norm=False,
                           sample0=0, n_samples=half, **paged)

    z = _norm_matmul(hp, norm_mix[1], gm_w_in, tm=tp, tn=1024, act="gelu")
    a = _gate_prompt(z, c_ln_g[0], c_ln_b[0], c_w_s[0], gm_bias, tm=plan.t_small)
    hp = _matmul_res(a, gm_w_out, hp, tm=tp, tn=1024)
    hp, att_b = _ffn_paged(hp, norm_ffn[1], w1_all, w2_all, norm_final, page_table, q_s, kn_s,
                           cache_latent, cache_kpe_t, mlp_layer=1, final_norm=True,
                           sample0=half, n_samples=n_dec - half, **paged)

    o_mla = _value_up(jnp.concatenate([att_a, att_b], axis=0), wuv_t)
    o_lru, lru_s, conv_s = _lru_step(
        proj_s, state_conv[0].reshape(n_dec, (CONV_W - 1) * LRU_WIDTH), state_lru_h[0], *lru_args)
    hs = _pair_matmul_res(o_mla, o_lru, mix_out, hs, tm=tsm, tn=1024)
    hs = _ffn(hs, norm_ffn[0], w1_all, w2_all, norm_final, layer=0, tm=tsm, tf=plan.tf, final_norm=False)

    z = _norm_matmul(hs, norm_mix[1], gm_w_in, tm=tsm, tn=1024, act="gelu")
    a, v_s = _gate_step(z, c_ln_g[0], c_ln_b[0], gm_w0, gm_bias[0])
    hs = _matmul_res(a, gm_w_out, hs, tm=tsm, tn=1024)
    hs = _ffn(hs, norm_ffn[1], w1_all, w2_all, norm_final, layer=1, tm=tsm, tf=plan.tf, final_norm=True)

    return (hp.reshape(batch, seq, D_MODEL), hs.reshape(n_dec, 1, D_MODEL),
            lat_p.reshape(1, batch, seq, KV_RANK), kpe_p.reshape(1, batch, seq, QK_ROPE),
            lru_p.reshape(1, batch, LRU_WIDTH), conv_p[None],
            lat_s.reshape(1, n_dec, 1, KV_RANK), kpe_s.reshape(1, n_dec, 1, QK_ROPE),
            lru_s[None], conv_s.reshape(1, n_dec, CONV_W - 1, LRU_WIDTH),
            v_s.reshape(1, n_dec, 1, GMLP_HALF))
```

```python
import functools
from typing import NamedTuple

import numpy as np
import jax
import jax.numpy as jnp
from jax import lax
from jax.experimental import pallas as pl
from jax.experimental.pallas import tpu as pltpu

D_MODEL = 2048
MLA_HEADS = 8
QK_NOPE = 128
QK_ROPE = 64
V_HEAD = 128
KV_RANK = 512
ROPE_THETA = 10000.0
ATTN_SCALE = (QK_NOPE + QK_ROPE) ** -0.5
LRU_WIDTH = D_MODEL // 2
LRU_HEADS = 8
LRU_HEAD_DIM = LRU_WIDTH // LRU_HEADS
CONV_W = 4
LRU_C = 8.0
GMLP_HALF = D_MODEL
CHUNK = 128
GMLP_GROUPS = 8
GMLP_GROUP_DIM = GMLP_HALF // GMLP_GROUPS
D_FF = 4 * D_MODEL
EPS = 1e-6
NEG_INF = -1e30
LOG2_E = 1.4426950408889634
PAGE_SIZE = 128

LANES = 128
V7X_VMEM_BYTES = 64 * 1024 * 1024
VMEM_LIMIT = (V7X_VMEM_BYTES * 7) // 8

FUSED_PAGE_BUFFERS = 4

QK_WIDTH = KV_RANK + LANES
HEAD_WIDTH = QK_NOPE + LANES

COL_QN = 0
COL_XB = 1024
COL_GB = 2048
COL_QP = 3072
COL_CKV = 3584
COL_KPE = 4096
PROJ_WIDTH = 4224

BF16 = jnp.bfloat16
F32 = jnp.float32


def _cparams(*sem):
    return pltpu.CompilerParams(dimension_semantics=sem, vmem_limit_bytes=VMEM_LIMIT)


def _lane_repeat(x, n):
    return jnp.concatenate([x] * n, axis=1)


def _rms(x, g):
    return x * lax.rsqrt(jnp.mean(x * x, axis=-1, keepdims=True) + EPS) * g


def _norm_matmul_kernel(x_ref, g_ref, w_ref, o_ref, xn_ref, *, act):
    @pl.when(pl.program_id(1) == 0)
    def _():
        xn_ref[...] = _rms(x_ref[...], g_ref[...]).astype(BF16)

    y = jnp.dot(xn_ref[...], w_ref[...], preferred_element_type=F32)
    if act == "gelu":
        y = jax.nn.gelu(y)
    o_ref[...] = y


def _norm_matmul(x, g, w, *, tm, tn, act=None):
    t, k = x.shape
    n = w.shape[1]
    return pl.pallas_call(
        functools.partial(_norm_matmul_kernel, act=act),
        grid=(t // tm, n // tn),
        in_specs=[
            pl.BlockSpec((tm, k), lambda i, j: (i, 0)),
            pl.BlockSpec((1, k), lambda i, j: (0, 0)),
            pl.BlockSpec((k, tn), lambda i, j: (0, j)),
        ],
        out_specs=pl.BlockSpec((tm, tn), lambda i, j: (i, j)),
        out_shape=jax.ShapeDtypeStruct((t, n), F32),
        scratch_shapes=[pltpu.VMEM((tm, k), BF16)],
        compiler_params=_cparams("parallel", "arbitrary"),
        name="norm_matmul",
    )(x, g.reshape(1, k), w)


def _pair_matmul_res_kernel(a_ref, b_ref, wa_ref, wb_ref, r_ref, o_ref):
    y = jnp.dot(a_ref[...], wa_ref[...], preferred_element_type=F32)
    y = y + jnp.dot(b_ref[...], wb_ref[...], preferred_element_type=F32)
    o_ref[...] = r_ref[...] + y


def _pair_matmul_res(a, b, w, res, *, tm, tn):
    t, ka = a.shape
    kb = b.shape[1]
    assert ka == kb and w.shape[0] == ka + kb
    n = w.shape[1]
    return pl.pallas_call(
        _pair_matmul_res_kernel,
        grid=(t // tm, n // tn),
        in_specs=[
            pl.BlockSpec((tm, ka), lambda i, j: (i, 0)),
            pl.BlockSpec((tm, kb), lambda i, j: (i, 0)),
            pl.BlockSpec((ka, tn), lambda i, j: (0, j)),
            pl.BlockSpec((kb, tn), lambda i, j: (1, j)),
            pl.BlockSpec((tm, tn), lambda i, j: (i, j)),
        ],
        out_specs=pl.BlockSpec((tm, tn), lambda i, j: (i, j)),
        out_shape=jax.ShapeDtypeStruct((t, n), F32),
        compiler_params=_cparams("parallel", "arbitrary"),
        name="pair_matmul_res",
    )(a, b, w, w, res)


def _ffn_step(x_ref, g_ref, w1_ref, w2_ref, gf_ref, o_ref, xn_ref, *, final_norm, col_chunk):
    j = pl.program_id(1)

    @pl.when(j == 0)
    def _():
        x = x_ref[...]
        xn_ref[...] = _rms(x, g_ref[...]).astype(BF16)
        o_ref[...] = x

    a = jnp.dot(xn_ref[...], w1_ref[...], preferred_element_type=F32)
    a = jnp.square(jnp.maximum(a, 0.0)).astype(BF16)
    for c in range(o_ref.shape[1] // col_chunk):
        sl = slice(c * col_chunk, (c + 1) * col_chunk)
        o_ref[:, sl] += jnp.dot(a, w2_ref[:, sl], preferred_element_type=F32)

    if final_norm:
        @pl.when(j == pl.num_programs(1) - 1)
        def _():
            o_ref[...] = _rms(o_ref[...], gf_ref[...])


def _ffn_kernel(x_ref, g_ref, w1_ref, w2_ref, gf_ref, o_ref, xn_ref, *, final_norm, col_chunk):
    _ffn_step(x_ref, g_ref, w1_ref, w2_ref, gf_ref, o_ref, xn_ref, final_norm=final_norm,
              col_chunk=col_chunk)


def _ffn(x, g, w1, w2, g_final, *, layer, tm, tf, final_norm):
    t, d = x.shape
    f = w1.shape[2]
    return pl.pallas_call(
        functools.partial(_ffn_kernel, final_norm=final_norm, col_chunk=512),
        grid=(t // tm, f // tf),
        in_specs=[
            pl.BlockSpec((tm, d), lambda i, j: (i, 0)),
            pl.BlockSpec((1, d), lambda i, j: (0, 0)),
            pl.BlockSpec((None, d, tf), lambda i, j: (layer, 0, j)),
            pl.BlockSpec((None, tf, d), lambda i, j: (layer, j, 0)),
            pl.BlockSpec((1, d), lambda i, j: (0, 0)),
        ],
        out_specs=pl.BlockSpec((tm, d), lambda i, j: (i, 0)),
        out_shape=jax.ShapeDtypeStruct((t, d), F32),
        scratch_shapes=[pltpu.VMEM((tm, d), BF16)],
        compiler_params=_cparams("parallel", "arbitrary"),
        name="ffn",
    )(x, g.reshape(1, d), w1, w2, g_final.reshape(1, d))


def _rope_fn(cos_ref, sin_ref):
    cos = cos_ref[...]
    sin = sin_ref[...]
    lane = lax.broadcasted_iota(jnp.int32, cos.shape, 1)
    first_half = (lane & (QK_ROPE - 1)) < (QK_ROPE // 2)

    def rope(x):
        rot = jnp.where(first_half,
                        pltpu.roll(x, LANES - QK_ROPE // 2, 1),
                        pltpu.roll(x, QK_ROPE // 2, 1))
        return x * cos + rot * sin

    return rope, lane < QK_ROPE


def _rope_queries(qp_ref, rope, low, scale):
    tiles = []
    for pair in range(MLA_HEADS // 2):
        xp = rope(qp_ref[:, pair * LANES:(pair + 1) * LANES]) * scale
        tiles.append(jnp.where(low, xp, 0.0).astype(BF16))
        tiles.append(jnp.where(low, pltpu.roll(xp, QK_ROPE, 1), 0.0).astype(BF16))
    return tiles


def _prep_kernel(qn_ref, qp_ref, ckv_ref, kpe_ref, cos_ref, sin_ref, gkv_ref, wuk_ref,
                 q_ref, k_ref, lat_ref, kpeo_ref):
    rope, low = _rope_fn(cos_ref, sin_ref)
    for h, tile in enumerate(_rope_queries(qp_ref, rope, low, ATTN_SCALE)):
        q_ref[h, :, KV_RANK:QK_WIDTH] = tile

    for h in range(MLA_HEADS):
        qn = qn_ref[:, h * QK_NOPE:(h + 1) * QK_NOPE].astype(BF16)
        ql = jnp.dot(qn, wuk_ref[h], preferred_element_type=F32) * ATTN_SCALE
        q_ref[h, :, 0:KV_RANK] = ql.astype(BF16)

    ckv = _rms(ckv_ref[...], gkv_ref[...])
    lat_ref[...] = ckv
    k_ref[:, 0:KV_RANK] = ckv.astype(BF16)
    kp = rope(kpe_ref[...])
    kpeo_ref[...] = kp[:, 0:QK_ROPE]
    k_ref[:, KV_RANK:QK_WIDTH] = kp.astype(BF16)


def _prep(proj, cos_tab, sin_tab, g_kv, wuk_t, *, tm):
    t = proj.shape[0]
    n_tab = cos_tab.shape[0] // tm

    def col(width, start):
        return pl.BlockSpec((tm, width), lambda i: (i, start // width))

    return pl.pallas_call(
        _prep_kernel,
        grid=(t // tm,),
        in_specs=[
            col(MLA_HEADS * QK_NOPE, COL_QN),
            col(MLA_HEADS * QK_ROPE, COL_QP),
            col(KV_RANK, COL_CKV),
            col(LANES, COL_KPE),
            pl.BlockSpec((tm, LANES), lambda i: (i % n_tab, 0)),
            pl.BlockSpec((tm, LANES), lambda i: (i % n_tab, 0)),
            pl.BlockSpec((1, KV_RANK), lambda i: (0, 0)),
            pl.BlockSpec((MLA_HEADS, QK_NOPE, KV_RANK), lambda i: (0, 0, 0)),
        ],
        out_specs=[
            pl.BlockSpec((MLA_HEADS, tm, QK_WIDTH), lambda i: (0, i, 0)),
            pl.BlockSpec((tm, QK_WIDTH), lambda i: (i, 0)),
            pl.BlockSpec((tm, KV_RANK), lambda i: (i, 0)),
            pl.BlockSpec((tm, QK_ROPE), lambda i: (i, 0)),
        ],
        out_shape=[
            jax.ShapeDtypeStruct((MLA_HEADS, t, QK_WIDTH), BF16),
            jax.ShapeDtypeStruct((t, QK_WIDTH), BF16),
            jax.ShapeDtypeStruct((t, KV_RANK), F32),
            jax.ShapeDtypeStruct((t, QK_ROPE), F32),
        ],
        compiler_params=_cparams("parallel"),
        name="mla_prep",
    )(proj, proj, proj, proj, cos_tab, sin_tab, g_kv.reshape(1, KV_RANK), wuk_t)


def _prep_prompt_kernel(qn_ref, qp_ref, ckv_ref, kpe_ref, cos_ref, sin_ref, gkv_ref, wuk_ref, wuv_ref,
                        q_ref, k_ref, v_ref, lat_ref, kpeo_ref):
    rope, low = _rope_fn(cos_ref, sin_ref)
    ckv = _rms(ckv_ref[...], gkv_ref[...])
    lat_ref[...] = ckv
    ckv_b = ckv.astype(BF16)
    kp = rope(kpe_ref[...])
    kpeo_ref[...] = kp[:, 0:QK_ROPE]
    kp_b = kp.astype(BF16)
    k_nope = jnp.dot(ckv_b, wuk_ref[...], preferred_element_type=F32)
    v_all = jnp.dot(ckv_b, wuv_ref[...], preferred_element_type=F32)
    scale = ATTN_SCALE * LOG2_E
    for h, tile in enumerate(_rope_queries(qp_ref, rope, low, scale)):
        nope = slice(h * QK_NOPE, (h + 1) * QK_NOPE)
        q_ref[h, :, 0:QK_NOPE] = (qn_ref[:, nope] * scale).astype(BF16)
        q_ref[h, :, QK_NOPE:HEAD_WIDTH] = tile
        k_ref[h, :, 0:QK_NOPE] = k_nope[:, nope].astype(BF16)
        k_ref[h, :, QK_NOPE:HEAD_WIDTH] = kp_b
        v_ref[h] = v_all[:, h * V_HEAD:(h + 1) * V_HEAD].astype(BF16)


def _prep_prompt(proj, cos_tab, sin_tab, g_kv, wuk, wuv, *, tm):
    t = proj.shape[0]
    n_tab = cos_tab.shape[0] // tm

    def col(width, start):
        return pl.BlockSpec((tm, width), lambda i: (i, start // width))

    full = lambda shape: pl.BlockSpec(shape, lambda i: (0,) * len(shape))
    head = lambda width: pl.BlockSpec((MLA_HEADS, tm, width), lambda i: (0, i, 0))
    return pl.pallas_call(
        _prep_prompt_kernel,
        grid=(t // tm,),
        in_specs=[
            col(MLA_HEADS * QK_NOPE, COL_QN),
            col(MLA_HEADS * QK_ROPE, COL_QP),
            col(KV_RANK, COL_CKV),
            col(LANES, COL_KPE),
            pl.BlockSpec((tm, LANES), lambda i: (i % n_tab, 0)),
            pl.BlockSpec((tm, LANES), lambda i: (i % n_tab, 0)),
            full((1, KV_RANK)),
            full((KV_RANK, MLA_HEADS * QK_NOPE)),
            full((KV_RANK, MLA_HEADS * V_HEAD)),
        ],
        out_specs=[
            head(HEAD_WIDTH), head(HEAD_WIDTH), head(V_HEAD),
            pl.BlockSpec((tm, KV_RANK), lambda i: (i, 0)),
            pl.BlockSpec((tm, QK_ROPE), lambda i: (i, 0)),
        ],
        out_shape=[
            jax.ShapeDtypeStruct((MLA_HEADS, t, HEAD_WIDTH), BF16),
            jax.ShapeDtypeStruct((MLA_HEADS, t, HEAD_WIDTH), BF16),
            jax.ShapeDtypeStruct((MLA_HEADS, t, V_HEAD), BF16),
            jax.ShapeDtypeStruct((t, KV_RANK), F32),
            jax.ShapeDtypeStruct((t, QK_ROPE), F32),
        ],
        compiler_params=_cparams("parallel"),
        name="mla_prep_prompt",
    )(proj, proj, proj, proj, cos_tab, sin_tab, g_kv.reshape(1, KV_RANK), wuk, wuv)


def _flash_kernel(qi_ref, ki_ref, q_ref, k_ref, v_ref, o_ref, m_ref, l_ref, acc_ref, *, tq, tk):
    p_id = pl.program_id(1)
    qi = qi_ref[p_id]
    ki = ki_ref[p_id]

    @pl.when(ki == 0)
    def _():
        m_ref[...] = jnp.full(m_ref.shape, NEG_INF, F32)
        l_ref[...] = jnp.zeros(l_ref.shape, F32)
        acc_ref[...] = jnp.zeros(acc_ref.shape, F32)

    def heads(masked):
        if masked:
            q_pos = qi * tq + lax.broadcasted_iota(jnp.int32, (tq, tk), 0)
            k_pos = ki * tk + lax.broadcasted_iota(jnp.int32, (tq, tk), 1)
            visible = k_pos <= q_pos
        for h in range(MLA_HEADS):
            s = lax.dot_general(q_ref[h], k_ref[h], (((1,), (1,)), ((), ())), preferred_element_type=F32)
            if masked:
                s = jnp.where(visible, s, NEG_INF)
            m_prev = m_ref[h]
            m_next = jnp.maximum(m_prev, jnp.max(s, axis=1, keepdims=True))
            alpha = jnp.exp2(m_prev - m_next)
            p = jnp.exp2(s - _lane_repeat(m_next, tk // LANES))
            l_ref[h] = alpha * l_ref[h] + jnp.sum(p, axis=1, keepdims=True)
            m_ref[h] = m_next
            acc_ref[h] = acc_ref[h] * alpha + jnp.dot(p.astype(BF16), v_ref[h],
                                                      preferred_element_type=F32)

    crosses_diagonal = (ki + 1) * tk - 1 > qi * tq
    pl.when(crosses_diagonal)(functools.partial(heads, True))
    pl.when(jnp.logical_not(crosses_diagonal))(functools.partial(heads, False))

    @pl.when((ki + 1) * tk >= (qi + 1) * tq)
    def _():
        for h in range(MLA_HEADS):
            o_ref[:, h * V_HEAD:(h + 1) * V_HEAD] = (acc_ref[h] / l_ref[h]).astype(BF16)


def _flash_prompt(q, k, v, *, batch, seq, tq, tk):
    assert seq % tq == 0 and seq % tk == 0
    nq = seq // tq
    qi_list, ki_list = [], []
    for qi in range(nq):
        for ki in range(-(-((qi + 1) * tq) // tk)):
            qi_list.append(qi)
            ki_list.append(ki)
    n_pairs = len(qi_list)
    qi_tab = jnp.asarray(np.asarray(qi_list, np.int32))
    ki_tab = jnp.asarray(np.asarray(ki_list, np.int32))
    nkb = seq // tk
    state = lambda: pltpu.VMEM((MLA_HEADS, tq, LANES), F32)
    grid_spec = pltpu.PrefetchScalarGridSpec(
        num_scalar_prefetch=2,
        grid=(batch, n_pairs),
        in_specs=[
            pl.BlockSpec((MLA_HEADS, tq, HEAD_WIDTH), lambda b, p, qt, kt: (0, b * nq + qt[p], 0)),
            pl.BlockSpec((MLA_HEADS, tk, HEAD_WIDTH), lambda b, p, qt, kt: (0, b * nkb + kt[p], 0)),
            pl.BlockSpec((MLA_HEADS, tk, V_HEAD), lambda b, p, qt, kt: (0, b * nkb + kt[p], 0)),
        ],
        out_specs=pl.BlockSpec((tq, MLA_HEADS * V_HEAD), lambda b, p, qt, kt: (b * nq + qt[p], 0)),
        scratch_shapes=[state(), state(), state()],
    )
    return pl.pallas_call(
        functools.partial(_flash_kernel, tq=tq, tk=tk),
        grid_spec=grid_spec,
        out_shape=jax.ShapeDtypeStruct((batch * seq, MLA_HEADS * V_HEAD), BF16),
        compiler_params=_cparams("parallel", "arbitrary"),
        name="flash_prompt",
    )(qi_tab, ki_tab, q, k, v)


def _expm1(y):
    return jnp.tanh(0.5 * y) * (jnp.exp(y) + 1.0)


def _lru_gates(xc, wa_ref, ba_ref, wx_ref, bx_ref, lam_ref):
    xcb = xc.astype(BF16)
    gr, gi = [], []
    for h in range(LRU_HEADS):
        xh = xcb[:, h * LRU_HEAD_DIM:(h + 1) * LRU_HEAD_DIM]
        gr.append(jnp.dot(xh, wa_ref[h], preferred_element_type=F32))
        gi.append(jnp.dot(xh, wx_ref[h], preferred_element_type=F32))
    gate_r = jnp.concatenate(gr, axis=1) + ba_ref[...]
    gate_i = jnp.concatenate(gi, axis=1) + bx_ref[...]
    r = jax.nn.sigmoid(gate_r)
    i = jax.nn.sigmoid(gate_i)
    log_a = -LRU_C * r * jax.nn.softplus(-lam_ref[...])
    a = jnp.exp(log_a)
    u = jnp.sqrt(-_expm1(2.0 * log_a)) * i * xc
    return a, u


def _lru_prompt_kernel(xb_ref, gb_ref, cw_ref, cb_ref, wa_ref, ba_ref, wx_ref, bx_ref, lam_ref,
                       o_ref, hlast_ref, conv_ref,
                       xpad_ref, a_ref, u_ref, hs_ref, h_ref):
    ts = xb_ref.shape[0]
    t_id = pl.program_id(1)
    halo = 8

    @pl.when(t_id == 0)
    def _():
        xpad_ref[0:halo, :] = jnp.zeros((halo, LRU_WIDTH), F32)
        h_ref[...] = jnp.zeros(h_ref.shape, F32)

    @pl.when(t_id > 0)
    def _():
        xpad_ref[0:halo, :] = xpad_ref[ts:ts + halo, :]

    xpad_ref[halo:halo + ts, :] = xb_ref[...]
    xc = cb_ref[...]
    for kk in range(CONV_W):
        off = halo - (CONV_W - 1) + kk
        xc = xc + xpad_ref[off:off + ts, :] * cw_ref[kk:kk + 1, :]
    conv_ref[0] = xpad_ref[ts + halo - (CONV_W - 1):ts + halo, :]

    a, u = _lru_gates(xc, wa_ref, ba_ref, wx_ref, bx_ref, lam_ref)
    a_ref[...] = a
    u_ref[...] = u

    def step(t, h):
        h = a_ref[pl.ds(t, 1), :] * h + u_ref[pl.ds(t, 1), :]
        hs_ref[pl.ds(t, 1), :] = h
        return h

    h = lax.fori_loop(0, ts, step, h_ref[0:1, :], unroll=8)
    h_ref[0:1, :] = h
    hlast_ref[0] = h
    o_ref[...] = (hs_ref[...] * jax.nn.gelu(gb_ref[...])).astype(BF16)


def _lru_prompt(proj, conv_w, conv_b, w_a, b_a, w_x, b_x, lam, *, batch, seq, ts):
    nt = seq // ts
    w = LRU_WIDTH
    vec = lambda: pl.BlockSpec((1, w), lambda b, t: (0, 0))
    hw = lambda: pl.BlockSpec((LRU_HEADS, LRU_HEAD_DIM, LRU_HEAD_DIM), lambda b, t: (0, 0, 0))
    return pl.pallas_call(
        _lru_prompt_kernel,
        grid=(batch, nt),
        in_specs=[
            pl.BlockSpec((ts, w), lambda b, t: (b * nt + t, COL_XB // w)),
            pl.BlockSpec((ts, w), lambda b, t: (b * nt + t, COL_GB // w)),
            pl.BlockSpec((CONV_W, w), lambda b, t: (0, 0)),
            vec(), hw(), vec(), hw(), vec(), vec(),
        ],
        out_specs=[
            pl.BlockSpec((ts, w), lambda b, t: (b * nt + t, 0)),
            pl.BlockSpec((1, 1, w), lambda b, t: (b, 0, 0)),
            pl.BlockSpec((1, CONV_W - 1, w), lambda b, t: (b, 0, 0)),
        ],
        out_shape=[
            jax.ShapeDtypeStruct((batch * seq, w), BF16),
            jax.ShapeDtypeStruct((batch, 1, w), F32),
            jax.ShapeDtypeStruct((batch, CONV_W - 1, w), F32),
        ],
        scratch_shapes=[
            pltpu.VMEM((ts + 8, w), F32),
            pltpu.VMEM((ts, w), F32),
            pltpu.VMEM((ts, w), F32),
            pltpu.VMEM((ts, w), F32),
            pltpu.VMEM((8, w), F32),
        ],
        compiler_params=_cparams("parallel", "arbitrary"),
        name="lru_prompt",
    )(proj, proj, conv_w, conv_b.reshape(1, w), w_a, b_a.reshape(1, w), w_x, b_x.reshape(1, w),
      lam.reshape(1, w))


def _lru_step_kernel(xb_ref, gb_ref, buf_ref, h0_ref, cw_ref, cb_ref, wa_ref, ba_ref, wx_ref, bx_ref,
                     lam_ref, o_ref, hnew_ref, conv_ref):
    w = LRU_WIDTH
    xb = xb_ref[...]
    xc = cb_ref[...]
    for kk in range(CONV_W - 1):
        xc = xc + buf_ref[:, kk * w:(kk + 1) * w] * cw_ref[kk:kk + 1, :]
    xc = xc + xb * cw_ref[CONV_W - 1:CONV_W, :]
    for kk in range(1, CONV_W - 1):
        conv_ref[:, (kk - 1) * w:kk * w] = buf_ref[:, kk * w:(kk + 1) * w]
    conv_ref[:, (CONV_W - 2) * w:(CONV_W - 1) * w] = xb
    a, u = _lru_gates(xc, wa_ref, ba_ref, wx_ref, bx_ref, lam_ref)
    h = a * h0_ref[...] + u
    hnew_ref[...] = h
    o_ref[...] = (h * jax.nn.gelu(gb_ref[...])).astype(BF16)


def _lru_step(proj, conv_buf, h0, conv_w, conv_b, w_a, b_a, w_x, b_x, lam):
    n = proj.shape[0]
    w = LRU_WIDTH
    full = lambda shape: pl.BlockSpec(shape, lambda i: (0,) * len(shape))
    return pl.pallas_call(
        _lru_step_kernel,
        grid=(1,),
        in_specs=[
            pl.BlockSpec((n, w), lambda i: (0, COL_XB // w)),
            pl.BlockSpec((n, w), lambda i: (0, COL_GB // w)),
            full((n, (CONV_W - 1) * w)), full((n, w)), full((CONV_W, w)), full((1, w)),
            full((LRU_HEADS, LRU_HEAD_DIM, LRU_HEAD_DIM)), full((1, w)),
            full((LRU_HEADS, LRU_HEAD_DIM, LRU_HEAD_DIM)), full((1, w)), full((1, w)),
        ],
        out_specs=[full((n, w)), full((n, w)), full((n, (CONV_W - 1) * w))],
        out_shape=[
            jax.ShapeDtypeStruct((n, w), BF16),
            jax.ShapeDtypeStruct((n, w), F32),
            jax.ShapeDtypeStruct((n, (CONV_W - 1) * w), F32),
        ],
        compiler_params=_cparams("arbitrary"),
        name="lru_step",
    )(proj, proj, conv_buf, h0, conv_w, conv_b.reshape(1, w), w_a, b_a.reshape(1, w), w_x,
      b_x.reshape(1, w), lam.reshape(1, w))


def _paged_share(step, pt_ref, q_ref, kn_ref, lat_hbm, kpe_hbm, o_ref,
                 latbuf, kpebuf, sem, latb_ref, s_ref, m_ref, l_ref, acc_ref,
                 *, layer, group, n_groups, sample0, n_samples, step_groups):
    total = n_samples * n_groups
    n_buf = latbuf.shape[0]
    g_base = step * step_groups

    def copies(g, slot):
        gc = jnp.minimum(g, total - 1)
        sb = sample0 + gc // n_groups
        sj = gc % n_groups
        out = []
        for i in range(group):
            page = pt_ref[sb, sj * group + i]
            keys = pl.ds(i * PAGE_SIZE, PAGE_SIZE)
            out.append(pltpu.make_async_copy(lat_hbm.at[layer, page], latbuf.at[slot, keys], sem.at[0, slot]))
            out.append(pltpu.make_async_copy(kpe_hbm.at[layer, page], kpebuf.at[slot, :, keys], sem.at[1, slot]))
        return out

    def wait_group(g):
        for c in copies(g, g % n_buf):
            c.wait()

    def stage_scores(g, parity):
        slot = g % n_buf
        q = q_ref[sample0 + jnp.minimum(g, total - 1) // n_groups]
        lat = latbuf[slot].astype(BF16)
        latb_ref[parity] = lat
        kpe = kpebuf[slot].astype(BF16)
        s = lax.dot_general(q[:, 0:KV_RANK], lat, (((1,), (1,)), ((), ())), preferred_element_type=F32)
        s = s + jnp.dot(q[:, KV_RANK:KV_RANK + QK_ROPE], kpe, preferred_element_type=F32)
        s_ref[parity] = s

    def softmax_step(s):
        m_prev = m_ref[...]
        m_next = jnp.maximum(m_prev, jnp.max(s, axis=1, keepdims=True))
        alpha = jnp.exp(m_prev - m_next)
        p = jnp.exp(s - m_next[:, 0:1])
        l_ref[...] = alpha * l_ref[...] + jnp.sum(p, axis=1, keepdims=True)
        m_ref[...] = m_next
        return p, alpha[:, 0:1]

    @pl.when(step == 0)
    def _():
        for g0 in range(n_buf):
            for c in copies(g0, g0):
                c.start()
        wait_group(0)
        stage_scores(0, 0)

    @pl.when(g_base % n_groups == 0)
    def _():
        m_ref[...] = jnp.full(m_ref.shape, NEG_INF, F32)
        l_ref[...] = jnp.zeros(l_ref.shape, F32)
        acc_ref[...] = jnp.zeros(acc_ref.shape, F32)

    def body(jj, carry):
        for parity in range(2):
            g = g_base + 2 * jj + parity
            wait_group(g + 1)

            @pl.when(g + n_buf <= total)
            def _():
                for c in copies(g + n_buf, g % n_buf):
                    c.start()

            p, alpha = softmax_step(s_ref[parity])
            stage_scores(g + 1, 1 - parity)
            pv = jnp.dot(p.astype(BF16), latb_ref[parity], preferred_element_type=F32)
            acc_ref[...] = acc_ref[...] * alpha + pv
        return carry

    lax.fori_loop(0, step_groups // 2, body, 0)

    @pl.when((g_base + step_groups) % n_groups == 0)
    def _():
        sample = sample0 + g_base // n_groups
        q = q_ref[sample]
        kn = kn_ref[pl.ds(sample, 1), :]
        s_new = jnp.sum(q.astype(F32) * kn, axis=1, keepdims=True)
        p, alpha = softmax_step(s_new)
        acc = acc_ref[...] * alpha + p.astype(BF16).astype(F32) * kn[:, 0:KV_RANK]
        o_ref[0] = acc / l_ref[:, 0:1]


def _ffn_paged_kernel(pt_ref, x_ref, g_ref, w1_ref, w2_ref, gf_ref, q_ref, kn_ref, lat_hbm, kpe_hbm,
                      o_ref, att_ref, xn_ref, latbuf, kpebuf, sem, latb_ref, s_ref, m_ref, l_ref, acc_ref,
                      *, final_norm, col_chunk, **paged):
    _ffn_step(x_ref, g_ref, w1_ref, w2_ref, gf_ref, o_ref, xn_ref, final_norm=final_norm,
              col_chunk=col_chunk)
    step = pl.program_id(0) * pl.num_programs(1) + pl.program_id(1)
    _paged_share(step, pt_ref, q_ref, kn_ref, lat_hbm, kpe_hbm, att_ref,
                 latbuf, kpebuf, sem, latb_ref, s_ref, m_ref, l_ref, acc_ref, **paged)


def _ffn_paged(x, g, w1, w2, g_final, page_table, q, k_new, cache_latent, cache_kpe_t, *,
               mlp_layer, tm, tf, final_norm, cache_layer, group, sample0, n_samples):
    t, d = x.shape
    f = w1.shape[2]
    ni, nj = t // tm, f // tf
    n_pages = page_table.shape[1]
    assert n_pages % group == 0
    n_groups = n_pages // group
    total = n_samples * n_groups
    assert total % (ni * nj) == 0
    step_groups = total // (ni * nj)
    assert step_groups % 2 == 0 and n_groups % step_groups == 0 and total >= FUSED_PAGE_BUFFERS
    keys = group * PAGE_SIZE
    n_dec = q.shape[0]
    once = pl.Buffered(1)

    def att_index(i, j, pt):
        return ((i * nj + j) * step_groups // n_groups, 0, 0)

    grid_spec = pltpu.PrefetchScalarGridSpec(
        num_scalar_prefetch=1,
        grid=(ni, nj),
        in_specs=[
            pl.BlockSpec((tm, d), lambda i, j, pt: (i, 0), pipeline_mode=once),
            pl.BlockSpec((1, d), lambda i, j, pt: (0, 0)),
            pl.BlockSpec((None, d, tf), lambda i, j, pt: (mlp_layer, 0, j)),
            pl.BlockSpec((None, tf, d), lambda i, j, pt: (mlp_layer, j, 0)),
            pl.BlockSpec((1, d), lambda i, j, pt: (0, 0)),
            pl.BlockSpec((n_dec, MLA_HEADS, QK_WIDTH), lambda i, j, pt: (0, 0, 0), pipeline_mode=once),
            pl.BlockSpec((n_dec, QK_WIDTH), lambda i, j, pt: (0, 0), pipeline_mode=once),
            pl.BlockSpec(memory_space=pl.ANY),
            pl.BlockSpec(memory_space=pl.ANY),
        ],
        out_specs=[
            pl.BlockSpec((tm, d), lambda i, j, pt: (i, 0), pipeline_mode=once),
            pl.BlockSpec((1, MLA_HEADS, KV_RANK), att_index),
        ],
        scratch_shapes=[
            pltpu.VMEM((tm, d), BF16),
            pltpu.VMEM((FUSED_PAGE_BUFFERS, keys, KV_RANK), F32),
            pltpu.VMEM((FUSED_PAGE_BUFFERS, QK_ROPE, keys), F32),
            pltpu.SemaphoreType.DMA((2, FUSED_PAGE_BUFFERS)),
            pltpu.VMEM((2, keys, KV_RANK), BF16),
            pltpu.VMEM((2, MLA_HEADS, keys), F32),
            pltpu.VMEM((MLA_HEADS, LANES), F32),
            pltpu.VMEM((MLA_HEADS, LANES), F32),
            pltpu.VMEM((MLA_HEADS, KV_RANK), F32),
        ],
    )
    return pl.pallas_call(
        functools.partial(_ffn_paged_kernel, final_norm=final_norm, col_chunk=512, layer=cache_layer,
                          group=group, n_groups=n_groups, sample0=sample0, n_samples=n_samples,
                          step_groups=step_groups),
        grid_spec=grid_spec,
        out_shape=[jax.ShapeDtypeStruct((t, d), F32),
                   jax.ShapeDtypeStruct((n_samples, MLA_HEADS, KV_RANK), F32)],
        compiler_params=_cparams("arbitrary", "arbitrary"),
        name="ffn_paged",
    )(page_table, x, g.reshape(1, d), w1, w2, g_final.reshape(1, d), q, k_new, cache_latent, cache_kpe_t)


def _value_up_kernel(o_ref, wuv_ref, y_ref):
    for h in range(MLA_HEADS):
        oh = o_ref[:, h * KV_RANK:(h + 1) * KV_RANK].astype(BF16)
        y_ref[:, h * V_HEAD:(h + 1) * V_HEAD] = jnp.dot(
            oh, wuv_ref[h], preferred_element_type=F32).astype(BF16)


def _value_up(o_lat, wuv_t):
    n = o_lat.shape[0]
    full = lambda shape: pl.BlockSpec(shape, lambda i: (0,) * len(shape))
    return pl.pallas_call(
        _value_up_kernel,
        grid=(1,),
        in_specs=[full((n, MLA_HEADS * KV_RANK)), full((MLA_HEADS, KV_RANK, V_HEAD))],
        out_specs=full((n, MLA_HEADS * V_HEAD)),
        out_shape=jax.ShapeDtypeStruct((n, MLA_HEADS * V_HEAD), BF16),
        compiler_params=_cparams("arbitrary"),
        name="value_up",
    )(o_lat.reshape(n, MLA_HEADS * KV_RANK), wuv_t)


def _layer_norm(v, g, b):
    mu = jnp.mean(v, axis=-1, keepdims=True)
    var = jnp.mean(jnp.square(v - mu), axis=-1, keepdims=True)
    return (v - mu) * lax.rsqrt(var + EPS) * g + b


def _gate_prompt_kernel(u_ref, v_ref, g_ref, b_ref, ws_ref, bias_ref, o_ref):
    tm = u_ref.shape[0]
    vb = _layer_norm(v_ref[...], g_ref[...], b_ref[...]).astype(BF16)
    row = lax.broadcasted_iota(jnp.int32, (CHUNK, CHUNK), 0)
    col = lax.broadcasted_iota(jnp.int32, (CHUNK, CHUNK), 1)
    causal = col <= row
    for g in range(GMLP_GROUPS):
        wc = jnp.where(causal, ws_ref[g], 0.0).astype(BF16)
        cs = slice(g * GMLP_GROUP_DIM, (g + 1) * GMLP_GROUP_DIM)
        for c in range(tm // CHUNK):
            rs = slice(c * CHUNK, (c + 1) * CHUNK)
            sg = jnp.dot(wc, vb[rs, cs], preferred_element_type=F32) + bias_ref[:, cs]
            o_ref[rs, cs] = (u_ref[rs, cs] * sg).astype(BF16)


def _gate_prompt(z, ln_g, ln_b, w_s, bias, *, tm):
    t = z.shape[0]
    d = GMLP_HALF
    full = lambda shape: pl.BlockSpec(shape, lambda i: (0,) * len(shape))
    return pl.pallas_call(
        _gate_prompt_kernel,
        grid=(t // tm,),
        in_specs=[
            pl.BlockSpec((tm, d), lambda i: (i, 0)),
            pl.BlockSpec((tm, d), lambda i: (i, 1)),
            full((1, d)), full((1, d)), full((GMLP_GROUPS, CHUNK, CHUNK)), full((CHUNK, d)),
        ],
        out_specs=pl.BlockSpec((tm, d), lambda i: (i, 0)),
        out_shape=jax.ShapeDtypeStruct((t, d), BF16),
        compiler_params=_cparams("parallel"),
        name="gate_prompt",
    )(z, z, ln_g.reshape(1, d), ln_b.reshape(1, d), w_s, bias)


def _gate_step_kernel(u_ref, v_ref, g_ref, b_ref, w0_ref, b0_ref, o_ref, vout_ref):
    v = _layer_norm(v_ref[...], g_ref[...], b_ref[...])
    vout_ref[...] = v
    sg = w0_ref[...].astype(BF16).astype(F32) * v.astype(BF16).astype(F32) + b0_ref[...]
    o_ref[...] = (u_ref[...] * sg).astype(BF16)


def _gate_step(z, ln_g, ln_b, w0, b0):
    n = z.shape[0]
    d = GMLP_HALF
    full = lambda shape: pl.BlockSpec(shape, lambda i: (0,) * len(shape))
    return pl.pallas_call(
        _gate_step_kernel,
        grid=(1,),
        in_specs=[
            pl.BlockSpec((n, d), lambda i: (0, 0)),
            pl.BlockSpec((n, d), lambda i: (0, 1)),
            full((1, d)), full((1, d)), full((1, d)), full((1, d)),
        ],
        out_specs=[full((n, d)), full((n, d))],
        out_shape=[jax.ShapeDtypeStruct((n, d), BF16), jax.ShapeDtypeStruct((n, d), F32)],
        compiler_params=_cparams("arbitrary"),
        name="gate_step",
    )(z, z, ln_g.reshape(1, d), ln_b.reshape(1, d), w0.reshape(1, d), b0.reshape(1, d))


def _matmul_res_kernel(a_ref, w_ref, r_ref, o_ref):
    o_ref[...] = r_ref[...] + jnp.dot(a_ref[...], w_ref[...], preferred_element_type=F32)


def _matmul_res(a, w, res, *, tm, tn):
    t, k = a.shape
    n = w.shape[1]
    return pl.pallas_call(
        _matmul_res_kernel,
        grid=(t // tm, n // tn),
        in_specs=[
            pl.BlockSpec((tm, k), lambda i, j: (i, 0)),
            pl.BlockSpec((k, tn), lambda i, j: (0, j)),
            pl.BlockSpec((tm, tn), lambda i, j: (i, j)),
        ],
        out_specs=pl.BlockSpec((tm, tn), lambda i, j: (i, j)),
        out_shape=jax.ShapeDtypeStruct((t, n), F32),
        compiler_params=_cparams("parallel", "arbitrary"),
        name="matmul_res",
    )(a, w, res)


def _pack_w_in(w_in):
    q_dim = MLA_HEADS * (QK_NOPE + QK_ROPE)
    q = w_in[:, :q_dim].reshape(D_MODEL, MLA_HEADS, QK_NOPE + QK_ROPE)
    qn = q[:, :, :QK_NOPE].reshape(D_MODEL, MLA_HEADS * QK_NOPE)
    qp = q[:, :, QK_NOPE:].reshape(D_MODEL, MLA_HEADS * QK_ROPE)
    ckv = w_in[:, q_dim:q_dim + KV_RANK]
    kpe = w_in[:, q_dim + KV_RANK:q_dim + KV_RANK + QK_ROPE]
    xb = w_in[:, q_dim + KV_RANK + QK_ROPE:q_dim + KV_RANK + QK_ROPE + LRU_WIDTH]
    gb = w_in[:, q_dim + KV_RANK + QK_ROPE + LRU_WIDTH:]
    pad = jnp.zeros((D_MODEL, LANES - QK_ROPE), w_in.dtype)
    packed = jnp.concatenate([qn, xb, gb, qp, ckv, kpe, pad], axis=1)
    assert packed.shape[1] == PROJ_WIDTH
    return packed.astype(BF16)


def _rope_tables(pos):
    half = QK_ROPE // 2
    inv = ROPE_THETA ** (-jnp.arange(half, dtype=F32) / half)
    ang = pos.astype(F32)[:, None] * inv[None, :]
    cos = jnp.cos(ang)
    sin = jnp.sin(ang)
    reps = LANES // QK_ROPE
    cos_tab = jnp.tile(jnp.concatenate([cos, cos], axis=1), (1, reps))
    sin_tab = jnp.tile(jnp.concatenate([-sin, sin], axis=1), (1, reps))
    return cos_tab, sin_tab


class _Plan(NamedTuple):
    tm_prompt: int
    tm_sample: int
    t_small: int
    tq: int
    tk: int
    page_group: int
    tf: int


def _plan(batch, seq, n_dec, n_pages):
    t = batch * seq
    tm_prompt = 1024 if t % 1024 == 0 else t
    t_small = 512 if seq % 512 == 0 else seq
    tk = 512 if seq % 512 == 0 else seq
    tq = 1024 if seq % 1024 == 0 else tk
    return _Plan(tm_prompt=tm_prompt, tm_sample=n_dec, t_small=t_small, tq=tq, tk=tk,
                 page_group=16 if n_pages % 16 == 0 else 2, tf=512)


def kernel(x_prompt, x_sample, cache_latent, cache_kpe, state_lru_h, state_conv, page_table,
           norm_mix, norm_ffn, norm_final, ab_w_in, ab_g_kv, ab_w_uk, ab_w_uv, ab_conv_w, ab_conv_b,
           ab_w_a, ab_b_a, ab_w_x, ab_b_x, ab_lambda, ab_w_out, c_w_in, c_ln_g, c_ln_b, c_w_s, c_b_s,
           c_w_out, ffn_w1, ffn_w2):
    batch, seq, _ = x_prompt.shape
    n_dec, dec_seq, _ = x_sample.shape
    assert dec_seq == 1
    n_pages = page_table.shape[1]
    past_len = n_pages * PAGE_SIZE
    depth = norm_mix.shape[0]

    cos_p, sin_p = _rope_tables(jnp.arange(seq, dtype=jnp.int32))
    cos_s, sin_s = _rope_tables(jnp.full((n_dec,), past_len, jnp.int32))
    cache_kpe_t = jnp.swapaxes(cache_kpe, 2, 3)
    w1_all = ffn_w1.astype(BF16)
    w2_all = ffn_w2.astype(BF16)

    hp = x_prompt.reshape(batch * seq, D_MODEL)
    hs = x_sample.reshape(n_dec, D_MODEL)
    plan = _plan(batch, seq, n_dec, n_pages)
    tp = plan.tm_prompt
    tsm = plan.tm_sample

    assert depth == 2
    mix_w = _pack_w_in(ab_w_in[0])
    wuk_t = jnp.transpose(ab_w_uk[0], (1, 2, 0)).astype(BF16)
    wuv_t = jnp.transpose(ab_w_uv[0], (1, 0, 2)).astype(BF16)
    mix_out = ab_w_out[0].astype(BF16)
    lru_args = (ab_conv_w[0], ab_conv_b[0], ab_w_a[0].astype(BF16), ab_b_a[0], ab_w_x[0].astype(BF16),
                ab_b_x[0], ab_lambda[0])
    gm_w_in = c_w_in[0].astype(BF16)
    gm_w_out = c_w_out[0].astype(BF16)
    gm_bias = jnp.repeat(c_b_s[0].T, GMLP_GROUP_DIM, axis=1)
    gm_w0 = jnp.repeat(c_w_s[0][:, 0, 0], GMLP_GROUP_DIM)

    proj_s = _norm_matmul(hs, norm_mix[0], mix_w, tm=tsm, tn=PROJ_WIDTH // 3)
    q_s, k_s, lat_s, kpe_s = _prep(proj_s, cos_s, sin_s, ab_g_kv[0], wuk_t, tm=tsm)
    q_s = jnp.transpose(q_s, (1, 0, 2))
    kn_s = k_s.astype(F32)
    half = n_dec // 2
    paged = dict(tm=tp, tf=plan.tf, cache_layer=0, group=plan.page_group)

    proj = _norm_matmul(hp, norm_mix[0], mix_w, tm=tp, tn=PROJ_WIDTH // 3)
    wuk_n = ab_w_uk[0].reshape(KV_RANK, MLA_HEADS * QK_NOPE).astype(BF16)
    wuv_n = ab_w_uv[0].reshape(KV_RANK, MLA_HEADS * V_HEAD).astype(BF16)
    q, k, v, lat_p, kpe_p = _prep_prompt(proj, cos_p, sin_p, ab_g_kv[0], wuk_n, wuv_n, tm=plan.t_small)
    o_mla = _flash_prompt(q, k, v, batch=batch, seq=seq, tq=plan.tq, tk=plan.tk)
    o_lru, lru_p, conv_p = _lru_prompt(proj, *lru_args, batch=batch, seq=seq, ts=plan.t_small)
    hp = _pair_matmul_res(o_mla, o_lru, mix_out, hp, tm=tp, tn=1024)
    hp, att_a = _ffn_paged(hp, norm_ffn[0], w1_all, w2_all, norm_final, page_table, q_s, kn_s,
                           cache_latent, cache_kpe_t, mlp_layer=0, final_norm=False,
                           sample0=0, n_samples=half, **paged)

    z = _norm_matmul(hp, norm_mix[1], gm_w_in, tm=tp, tn=1024, act="gelu")
    a = _gate_prompt(z, c_ln_g[0], c_ln_b[0], c_w_s[0], gm_bias, tm=plan.t_small)
    hp = _matmul_res(a, gm_w_out, hp, tm=tp, tn=1024)
    hp, att_b = _ffn_paged(hp, norm_ffn[1], w1_all, w2_all, norm_final, page_table, q_s, kn_s,
                           cache_latent, cache_kpe_t, mlp_layer=1, final_norm=True,
                           sample0=half, n_samples=n_dec - half, **paged)

    o_mla = _value_up(jnp.concatenate([att_a, att_b], axis=0), wuv_t)
    o_lru, lru_s, conv_s = _lru_step(
        proj_s, state_conv[0].reshape(n_dec, (CONV_W - 1) * LRU_WIDTH), state_lru_h[0], *lru_args)
    hs = _pair_matmul_res(o_mla, o_lru, mix_out, hs, tm=tsm, tn=1024)
    hs = _ffn(hs, norm_ffn[0], w1_all, w2_all, norm_final, layer=0, tm=tsm, tf=plan.tf, final_norm=False)

    z = _norm_matmul(hs, norm_mix[1], gm_w_in, tm=tsm, tn=1024, act="gelu")
    a, v_s = _gate_step(z, c_ln_g[0], c_ln_b[0], gm_w0, gm_bias[0])
    hs = _matmul_res(a, gm_w_out, hs, tm=tsm, tn=1024)
    hs = _ffn(hs, norm_ffn[1], w1_all, w2_all, norm_final, layer=1, tm=tsm, tf=plan.tf, final_norm=True)

    return (hp.reshape(batch, seq, D_MODEL), hs.reshape(n_dec, 1, D_MODEL),
            lat_p.reshape(1, batch, seq, KV_RANK), kpe_p.reshape(1, batch, seq, QK_ROPE),
            lru_p.reshape(1, batch, LRU_WIDTH), conv_p[None],
            lat_s.reshape(1, n_dec, 1, KV_RANK), kpe_s.reshape(1, n_dec, 1, QK_ROPE),
            lru_s[None], conv_s.reshape(1, n_dec, CONV_W - 1, LRU_WIDTH),
            v_s.reshape(1, n_dec, 1, GMLP_HALF))
```

```python
import functools
from typing import NamedTuple

import numpy as np
import jax
import jax.numpy as jnp
from jax import lax
from jax.experimental import pallas as pl
from jax.experimental.pallas import tpu as pltpu

D_MODEL = 2048
MLA_HEADS = 8
QK_NOPE = 128
QK_ROPE = 64
V_HEAD = 128
KV_RANK = 512
ROPE_THETA = 10000.0
ATTN_SCALE = (QK_NOPE + QK_ROPE) ** -0.5
LRU_WIDTH = D_MODEL // 2
LRU_HEADS = 8
LRU_HEAD_DIM = LRU_WIDTH // LRU_HEADS
CONV_W = 4
LRU_C = 8.0
GMLP_HALF = D_MODEL
CHUNK = 128
GMLP_GROUPS = 8
GMLP_GROUP_DIM = GMLP_HALF // GMLP_GROUPS
D_FF = 4 * D_MODEL
EPS = 1e-6
NEG_INF = -1e30
LOG2_E = 1.4426950408889634
PAGE_SIZE = 128

LANES = 128
V7X_VMEM_BYTES = 64 * 1024 * 1024
VMEM_LIMIT = (V7X_VMEM_BYTES * 7) // 8

FUSED_PAGE_BUFFERS = 4

QK_WIDTH = KV_RANK + LANES
HEAD_WIDTH = QK_NOPE + LANES

COL_QN = 0
COL_XB = 1024
COL_GB = 2048
COL_QP = 3072
COL_CKV = 3584
COL_KPE = 4096
PROJ_WIDTH = 4224

BF16 = jnp.bfloat16
F32 = jnp.float32


def _cparams(*sem):
    return pltpu.CompilerParams(dimension_semantics=sem, vmem_limit_bytes=VMEM_LIMIT)


def _lane_repeat(x, n):
    return jnp.concatenate([x] * n, axis=1)


def _rms(x, g):
    return x * lax.rsqrt(jnp.mean(x * x, axis=-1, keepdims=True) + EPS) * g


def _norm_matmul_kernel(x_ref, g_ref, w_ref, o_ref, xn_ref, *, act):
    @pl.when(pl.program_id(1) == 0)
    def _():
        xn_ref[...] = _rms(x_ref[...], g_ref[...]).astype(BF16)

    y = jnp.dot(xn_ref[...], w_ref[...], preferred_element_type=F32)
    if act == "gelu":
        y = jax.nn.gelu(y)
    o_ref[...] = y


def _norm_matmul(x, g, w, *, tm, tn, act=None):
    t, k = x.shape
    n = w.shape[1]
    return pl.pallas_call(
        functools.partial(_norm_matmul_kernel, act=act),
        grid=(t // tm, n // tn),
        in_specs=[
            pl.BlockSpec((tm, k), lambda i, j: (i, 0)),
            pl.BlockSpec((1, k), lambda i, j: (0, 0)),
            pl.BlockSpec((k, tn), lambda i, j: (0, j)),
        ],
        out_specs=pl.BlockSpec((tm, tn), lambda i, j: (i, j)),
        out_shape=jax.ShapeDtypeStruct((t, n), F32),
        scratch_shapes=[pltpu.VMEM((tm, k), BF16)],
        compiler_params=_cparams("parallel", "arbitrary"),
        name="norm_matmul",
    )(x, g.reshape(1, k), w)


def _pair_matmul_res_kernel(a_ref, b_ref, wa_ref, wb_ref, r_ref, o_ref):
    y = jnp.dot(a_ref[...], wa_ref[...], preferred_element_type=F32)
    y = y + jnp.dot(b_ref[...], wb_ref[...], preferred_element_type=F32)
    o_ref[...] = r_ref[...] + y


def _pair_matmul_res(a, b, w, res, *, tm, tn):
    t, ka = a.shape
    kb = b.shape[1]
    assert ka == kb and w.shape[0] == ka + kb
    n = w.shape[1]
    return pl.pallas_call(
        _pair_matmul_res_kernel,
        grid=(t // tm, n // tn),
        in_specs=[
            pl.BlockSpec((tm, ka), lambda i, j: (i, 0)),
            pl.BlockSpec((tm, kb), lambda i, j: (i, 0)),
            pl.BlockSpec((ka, tn), lambda i, j: (0, j)),
            pl.BlockSpec((kb, tn), lambda i, j: (1, j)),
            pl.BlockSpec((tm, tn), lambda i, j: (i, j)),
        ],
        out_specs=pl.BlockSpec((tm, tn), lambda i, j: (i, j)),
        out_shape=jax.ShapeDtypeStruct((t, n), F32),
        compiler_params=_cparams("parallel", "arbitrary"),
        name="pair_matmul_res",
    )(a, b, w, w, res)


def _ffn_step(x_ref, g_ref, w1_ref, w2_ref, gf_ref, o_ref, xn_ref, *, final_norm, col_chunk):
    j = pl.program_id(1)

    @pl.when(j == 0)
    def _():
        x = x_ref[...]
        xn_ref[...] = _rms(x, g_ref[...]).astype(BF16)
        o_ref[...] = x

    a = jnp.dot(xn_ref[...], w1_ref[...], preferred_element_type=F32)
    a = jnp.square(jnp.maximum(a, 0.0)).astype(BF16)
    for c in range(o_ref.shape[1] // col_chunk):
        sl = slice(c * col_chunk, (c + 1) * col_chunk)
        o_ref[:, sl] += jnp.dot(a, w2_ref[:, sl], preferred_element_type=F32)

    if final_norm:
        @pl.when(j == pl.num_programs(1) - 1)
        def _():
            o_ref[...] = _rms(o_ref[...], gf_ref[...])


def _ffn_kernel(x_ref, g_ref, w1_ref, w2_ref, gf_ref, o_ref, xn_ref, *, final_norm, col_chunk):
    _ffn_step(x_ref, g_ref, w1_ref, w2_ref, gf_ref, o_ref, xn_ref, final_norm=final_norm,
              col_chunk=col_chunk)


def _ffn(x, g, w1, w2, g_final, *, layer, tm, tf, final_norm):
    t, d = x.shape
    f = w1.shape[2]
    return pl.pallas_call(
        functools.partial(_ffn_kernel, final_norm=final_norm, col_chunk=512),
        grid=(t // tm, f // tf),
        in_specs=[
            pl.BlockSpec((tm, d), lambda i, j: (i, 0)),
            pl.BlockSpec((1, d), lambda i, j: (0, 0)),
            pl.BlockSpec((None, d, tf), lambda i, j: (layer, 0, j)),
            pl.BlockSpec((None, tf, d), lambda i, j: (layer, j, 0)),
            pl.BlockSpec((1, d), lambda i, j: (0, 0)),
        ],
        out_specs=pl.BlockSpec((tm, d), lambda i, j: (i, 0)),
        out_shape=jax.ShapeDtypeStruct((t, d), F32),
        scratch_shapes=[pltpu.VMEM((tm, d), BF16)],
        compiler_params=_cparams("parallel", "arbitrary"),
        name="ffn",
    )(x, g.reshape(1, d), w1, w2, g_final.reshape(1, d))


def _rope_fn(cos_ref, sin_ref):
    cos = cos_ref[...]
    sin = sin_ref[...]
    lane = lax.broadcasted_iota(jnp.int32, cos.shape, 1)
    first_half = (lane & (QK_ROPE - 1)) < (QK_ROPE // 2)

    def rope(x):
        rot = jnp.where(first_half,
                        pltpu.roll(x, LANES - QK_ROPE // 2, 1),
                        pltpu.roll(x, QK_ROPE // 2, 1))
        return x * cos + rot * sin

    return rope, lane < QK_ROPE


def _rope_queries(qp_ref, rope, low, scale):
    tiles = []
    for pair in range(MLA_HEADS // 2):
        xp = rope(qp_ref[:, pair * LANES:(pair + 1) * LANES]) * scale
        tiles.append(jnp.where(low, xp, 0.0).astype(BF16))
        tiles.append(jnp.where(low, pltpu.roll(xp, QK_ROPE, 1), 0.0).astype(BF16))
    return tiles


def _prep_kernel(qn_ref, qp_ref, ckv_ref, kpe_ref, cos_ref, sin_ref, gkv_ref, wuk_ref,
                 q_ref, k_ref, lat_ref, kpeo_ref):
    rope, low = _rope_fn(cos_ref, sin_ref)
    for h, tile in enumerate(_rope_queries(qp_ref, rope, low, ATTN_SCALE)):
        q_ref[h, :, KV_RANK:QK_WIDTH] = tile

    for h in range(MLA_HEADS):
        qn = qn_ref[:, h * QK_NOPE:(h + 1) * QK_NOPE].astype(BF16)
        ql = jnp.dot(qn, wuk_ref[h], preferred_element_type=F32) * ATTN_SCALE
        q_ref[h, :, 0:KV_RANK] = ql.astype(BF16)

    ckv = _rms(ckv_ref[...], gkv_ref[...])
    lat_ref[...] = ckv
    k_ref[:, 0:KV_RANK] = ckv.astype(BF16)
    kp = rope(kpe_ref[...])
    kpeo_ref[...] = kp[:, 0:QK_ROPE]
    k_ref[:, KV_RANK:QK_WIDTH] = kp.astype(BF16)


def _prep(proj, cos_tab, sin_tab, g_kv, wuk_t, *, tm):
    t = proj.shape[0]
    n_tab = cos_tab.shape[0] // tm

    def col(width, start):
        return pl.BlockSpec((tm, width), lambda i: (i, start // width))

    return pl.pallas_call(
        _prep_kernel,
        grid=(t // tm,),
        in_specs=[
            col(MLA_HEADS * QK_NOPE, COL_QN),
            col(MLA_HEADS * QK_ROPE, COL_QP),
            col(KV_RANK, COL_CKV),
            col(LANES, COL_KPE),
            pl.BlockSpec((tm, LANES), lambda i: (i % n_tab, 0)),
            pl.BlockSpec((tm, LANES), lambda i: (i % n_tab, 0)),
            pl.BlockSpec((1, KV_RANK), lambda i: (0, 0)),
            pl.BlockSpec((MLA_HEADS, QK_NOPE, KV_RANK), lambda i: (0, 0, 0)),
        ],
        out_specs=[
            pl.BlockSpec((MLA_HEADS, tm, QK_WIDTH), lambda i: (0, i, 0)),
            pl.BlockSpec((tm, QK_WIDTH), lambda i: (i, 0)),
            pl.BlockSpec((tm, KV_RANK), lambda i: (i, 0)),
            pl.BlockSpec((tm, QK_ROPE), lambda i: (i, 0)),
        ],
        out_shape=[
            jax.ShapeDtypeStruct((MLA_HEADS, t, QK_WIDTH), BF16),
            jax.ShapeDtypeStruct((t, QK_WIDTH), BF16),
            jax.ShapeDtypeStruct((t, KV_RANK), F32),
            jax.ShapeDtypeStruct((t, QK_ROPE), F32),
        ],
        compiler_params=_cparams("parallel"),
        name="mla_prep",
    )(proj, proj, proj, proj, cos_tab, sin_tab, g_kv.reshape(1, KV_RANK), wuk_t)


def _prep_prompt_kernel(qn_ref, qp_ref, ckv_ref, kpe_ref, cos_ref, sin_ref, gkv_ref, wuk_ref, wuv_ref,
                        q_ref, k_ref, v_ref, lat_ref, kpeo_ref):
    rope, low = _rope_fn(cos_ref, sin_ref)
    ckv = _rms(ckv_ref[...], gkv_ref[...])
    lat_ref[...] = ckv
    ckv_b = ckv.astype(BF16)
    kp = rope(kpe_ref[...])
    kpeo_ref[...] = kp[:, 0:QK_ROPE]
    kp_b = kp.astype(BF16)
    k_nope = jnp.dot(ckv_b, wuk_ref[...], preferred_element_type=F32)
    v_all = jnp.dot(ckv_b, wuv_ref[...], preferred_element_type=F32)
    scale = ATTN_SCALE * LOG2_E
    for h, tile in enumerate(_rope_queries(qp_ref, rope, low, scale)):
        nope = slice(h * QK_NOPE, (h + 1) * QK_NOPE)
        q_ref[h, :, 0:QK_NOPE] = (qn_ref[:, nope] * scale).astype(BF16)
        q_ref[h, :, QK_NOPE:HEAD_WIDTH] = tile
        k_ref[h, :, 0:QK_NOPE] = k_nope[:, nope].astype(BF16)
        k_ref[h, :, QK_NOPE:HEAD_WIDTH] = kp_b
        v_ref[h] = v_all[:, h * V_HEAD:(h + 1) * V_HEAD].astype(BF16)


def _prep_prompt(proj, cos_tab, sin_tab, g_kv, wuk, wuv, *, tm):
    t = proj.shape[0]
    n_tab = cos_tab.shape[0] // tm

    def col(width, start):
        return pl.BlockSpec((tm, width), lambda i: (i, start // width))

    full = lambda shape: pl.BlockSpec(shape, lambda i: (0,) * len(shape))
    head = lambda width: pl.BlockSpec((MLA_HEADS, tm, width), lambda i: (0, i, 0))
    return pl.pallas_call(
        _prep_prompt_kernel,
        grid=(t // tm,),
        in_specs=[
            col(MLA_HEADS * QK_NOPE, COL_QN),
            col(MLA_HEADS * QK_ROPE, COL_QP),
            col(KV_RANK, COL_CKV),
            col(LANES, COL_KPE),
            pl.BlockSpec((tm, LANES), lambda i: (i % n_tab, 0)),
            pl.BlockSpec((tm, LANES), lambda i: (i % n_tab, 0)),
            full((1, KV_RANK)),
            full((KV_RANK, MLA_HEADS * QK_NOPE)),
            full((KV_RANK, MLA_HEADS * V_HEAD)),
        ],
        out_specs=[
            head(HEAD_WIDTH), head(HEAD_WIDTH), head(V_HEAD),
            pl.BlockSpec((tm, KV_RANK), lambda i: (i, 0)),
            pl.BlockSpec((tm, QK_ROPE), lambda i: (i, 0)),
        ],
        out_shape=[
            jax.ShapeDtypeStruct((MLA_HEADS, t, HEAD_WIDTH), BF16),
            jax.ShapeDtypeStruct((MLA_HEADS, t, HEAD_WIDTH), BF16),
            jax.ShapeDtypeStruct((MLA_HEADS, t, V_HEAD), BF16),
            jax.ShapeDtypeStruct((t, KV_RANK), F32),
            jax.ShapeDtypeStruct((t, QK_ROPE), F32),
        ],
        compiler_params=_cparams("parallel"),
        name="mla_prep_prompt",
    )(proj, proj, proj, proj, cos_tab, sin_tab, g_kv.reshape(1, KV_RANK), wuk, wuv)


def _flash_kernel(qi_ref, ki_ref, q_ref, k_ref, v_ref, o_ref, m_ref, l_ref, acc_ref, *, tq, tk):
    p_id = pl.program_id(1)
    qi = qi_ref[p_id]
    ki = ki_ref[p_id]

    @pl.when(ki == 0)
    def _():
        m_ref[...] = jnp.full(m_ref.shape, NEG_INF, F32)
        l_ref[...] = jnp.zeros(l_ref.shape, F32)
        acc_ref[...] = jnp.zeros(acc_ref.shape, F32)

    def heads(masked):
        if masked:
            q_pos = qi * tq + lax.broadcasted_iota(jnp.int32, (tq, tk), 0)
            k_pos = ki * tk + lax.broadcasted_iota(jnp.int32, (tq, tk), 1)
            visible = k_pos <= q_pos
        for h in range(MLA_HEADS):
            s = lax.dot_general(q_ref[h], k_ref[h], (((1,), (1,)), ((), ())), preferred_element_type=F32)
            if masked:
                s = jnp.where(visible, s, NEG_INF)
            m_prev = m_ref[h]
            m_next = jnp.maximum(m_prev, jnp.max(s, axis=1, keepdims=True))
            alpha = jnp.exp2(m_prev - m_next)
            p = jnp.exp2(s - _lane_repeat(m_next, tk // LANES))
            l_ref[h] = alpha * l_ref[h] + jnp.sum(p, axis=1, keepdims=True)
            m_ref[h] = m_next
            acc_ref[h] = acc_ref[h] * alpha + jnp.dot(p.astype(BF16), v_ref[h],
                                                      preferred_element_type=F32)

    crosses_diagonal = (ki + 1) * tk - 1 > qi * tq
    pl.when(crosses_diagonal)(functools.partial(heads, True))
    pl.when(jnp.logical_not(crosses_diagonal))(functools.partial(heads, False))

    @pl.when((ki + 1) * tk >= (qi + 1) * tq)
    def _():
        for h in range(MLA_HEADS):
            o_ref[:, h * V_HEAD:(h + 1) * V_HEAD] = (acc_ref[h] / l_ref[h]).astype(BF16)


def _flash_prompt(q, k, v, *, batch, seq, tq, tk):
    assert seq % tq == 0 and seq % tk == 0
    nq = seq // tq
    qi_list, ki_list = [], []
    for qi in range(nq):
        for ki in range(-(-((qi + 1) * tq) // tk)):
            qi_list.append(qi)
            ki_list.append(ki)
    n_pairs = len(qi_list)
    qi_tab = jnp.asarray(np.asarray(qi_list, np.int32))
    ki_tab = jnp.asarray(np.asarray(ki_list, np.int32))
    nkb = seq // tk
    state = lambda: pltpu.VMEM((MLA_HEADS, tq, LANES), F32)
    grid_spec = pltpu.PrefetchScalarGridSpec(
        num_scalar_prefetch=2,
        grid=(batch, n_pairs),
        in_specs=[
            pl.BlockSpec((MLA_HEADS, tq, HEAD_WIDTH), lambda b, p, qt, kt: (0, b * nq + qt[p], 0)),
            pl.BlockSpec((MLA_HEADS, tk, HEAD_WIDTH), lambda b, p, qt, kt: (0, b * nkb + kt[p], 0)),
            pl.BlockSpec((MLA_HEADS, tk, V_HEAD), lambda b, p, qt, kt: (0, b * nkb + kt[p], 0)),
        ],
        out_specs=pl.BlockSpec((tq, MLA_HEADS * V_HEAD), lambda b, p, qt, kt: (b * nq + qt[p], 0)),
        scratch_shapes=[state(), state(), state()],
    )
    return pl.pallas_call(
        functools.partial(_flash_kernel, tq=tq, tk=tk),
        grid_spec=grid_spec,
        out_shape=jax.ShapeDtypeStruct((batch * seq, MLA_HEADS * V_HEAD), BF16),
        compiler_params=_cparams("parallel", "arbitrary"),
        name="flash_prompt",
    )(qi_tab, ki_tab, q, k, v)


def _expm1(y):
    return jnp.tanh(0.5 * y) * (jnp.exp(y) + 1.0)


def _lru_gates(xc, wa_ref, ba_ref, wx_ref, bx_ref, lam_ref):
    xcb = xc.astype(BF16)
    gr, gi = [], []
    for h in range(LRU_HEADS):
        xh = xcb[:, h * LRU_HEAD_DIM:(h + 1) * LRU_HEAD_DIM]
        gr.append(jnp.dot(xh, wa_ref[h], preferred_element_type=F32))
        gi.append(jnp.dot(xh, wx_ref[h], preferred_element_type=F32))
    gate_r = jnp.concatenate(gr, axis=1) + ba_ref[...]
    gate_i = jnp.concatenate(gi, axis=1) + bx_ref[...]
    r = jax.nn.sigmoid(gate_r)
    i = jax.nn.sigmoid(gate_i)
    log_a = -LRU_C * r * jax.nn.softplus(-lam_ref[...])
    a = jnp.exp(log_a)
    u = jnp.sqrt(-_expm1(2.0 * log_a)) * i * xc
    return a, u


def _lru_prompt_kernel(xb_ref, gb_ref, cw_ref, cb_ref, wa_ref, ba_ref, wx_ref, bx_ref, lam_ref,
                       o_ref, hlast_ref, conv_ref,
                       xpad_ref, a_ref, u_ref, hs_ref, h_ref):
    ts = xb_ref.shape[0]
    t_id = pl.program_id(1)
    halo = 8

    @pl.when(t_id == 0)
    def _():
        xpad_ref[0:halo, :] = jnp.zeros((halo, LRU_WIDTH), F32)
        h_ref[...] = jnp.zeros(h_ref.shape, F32)

    @pl.when(t_id > 0)
    def _():
        xpad_ref[0:halo, :] = xpad_ref[ts:ts + halo, :]

    xpad_ref[halo:halo + ts, :] = xb_ref[...]
    xc = cb_ref[...]
    for kk in range(CONV_W):
        off = halo - (CONV_W - 1) + kk
        xc = xc + xpad_ref[off:off + ts, :] * cw_ref[kk:kk + 1, :]
    conv_ref[0] = xpad_ref[ts + halo - (CONV_W - 1):ts + halo, :]

    a, u = _lru_gates(xc, wa_ref, ba_ref, wx_ref, bx_ref, lam_ref)
    a_ref[...] = a
    u_ref[...] = u

    def step(t, h):
        h = a_ref[pl.ds(t, 1), :] * h + u_ref[pl.ds(t, 1), :]
        hs_ref[pl.ds(t, 1), :] = h
        return h

    h = lax.fori_loop(0, ts, step, h_ref[0:1, :], unroll=8)
    h_ref[0:1, :] = h
    hlast_ref[0] = h
    o_ref[...] = (hs_ref[...] * jax.nn.gelu(gb_ref[...])).astype(BF16)


def _lru_prompt(proj, conv_w, conv_b, w_a, b_a, w_x, b_x, lam, *, batch, seq, ts):
    nt = seq // ts
    w = LRU_WIDTH
    vec = lambda: pl.BlockSpec((1, w), lambda b, t: (0, 0))
    hw = lambda: pl.BlockSpec((LRU_HEADS, LRU_HEAD_DIM, LRU_HEAD_DIM), lambda b, t: (0, 0, 0))
    return pl.pallas_call(
        _lru_prompt_kernel,
        grid=(batch, nt),
        in_specs=[
            pl.BlockSpec((ts, w), lambda b, t: (b * nt + t, COL_XB // w)),
            pl.BlockSpec((ts, w), lambda b, t: (b * nt + t, COL_GB // w)),
            pl.BlockSpec((CONV_W, w), lambda b, t: (0, 0)),
            vec(), hw(), vec(), hw(), vec(), vec(),
        ],
        out_specs=[
            pl.BlockSpec((ts, w), lambda b, t: (b * nt + t, 0)),
            pl.BlockSpec((1, 1, w), lambda b, t: (b, 0, 0)),
            pl.BlockSpec((1, CONV_W - 1, w), lambda b, t: (b, 0, 0)),
        ],
        out_shape=[
            jax.ShapeDtypeStruct((batch * seq, w), BF16),
            jax.ShapeDtypeStruct((batch, 1, w), F32),
            jax.ShapeDtypeStruct((batch, CONV_W - 1, w), F32),
        ],
        scratch_shapes=[
            pltpu.VMEM((ts + 8, w), F32),
            pltpu.VMEM((ts, w), F32),
            pltpu.VMEM((ts, w), F32),
            pltpu.VMEM((ts, w), F32),
            pltpu.VMEM((8, w), F32),
        ],
        compiler_params=_cparams("parallel", "arbitrary"),
        name="lru_prompt",
    )(proj, proj, conv_w, conv_b.reshape(1, w), w_a, b_a.reshape(1, w), w_x, b_x.reshape(1, w),
      lam.reshape(1, w))


def _lru_step_kernel(xb_ref, gb_ref, buf_ref, h0_ref, cw_ref, cb_ref, wa_ref, ba_ref, wx_ref, bx_ref,
                     lam_ref, o_ref, hnew_ref, conv_ref):
    w = LRU_WIDTH
    xb = xb_ref[...]
    xc = cb_ref[...]
    for kk in range(CONV_W - 1):
        xc = xc + buf_ref[:, kk * w:(kk + 1) * w] * cw_ref[kk:kk + 1, :]
    xc = xc + xb * cw_ref[CONV_W - 1:CONV_W, :]
    for kk in range(1, CONV_W - 1):
        conv_ref[:, (kk - 1) * w:kk * w] = buf_ref[:, kk * w:(kk + 1) * w]
    conv_ref[:, (CONV_W - 2) * w:(CONV_W - 1) * w] = xb
    a, u = _lru_gates(xc, wa_ref, ba_ref, wx_ref, bx_ref, lam_ref)
    h = a * h0_ref[...] + u
    hnew_ref[...] = h
    o_ref[...] = (h * jax.nn.gelu(gb_ref[...])).astype(BF16)


def _lru_step(proj, conv_buf, h0, conv_w, conv_b, w_a, b_a, w_x, b_x, lam):
    n = proj.shape[0]
    w = LRU_WIDTH
    full = lambda shape: pl.BlockSpec(shape, lambda i: (0,) * len(shape))
    return pl.pallas_call(
        _lru_step_kernel,
        grid=(1,),
        in_specs=[
            pl.BlockSpec((n, w), lambda i: (0, COL_XB // w)),
            pl.BlockSpec((n, w), lambda i: (0, COL_GB // w)),
            full((n, (CONV_W - 1) * w)), full((n, w)), full((CONV_W, w)), full((1, w)),
            full((LRU_HEADS, LRU_HEAD_DIM, LRU_HEAD_DIM)), full((1, w)),
            full((LRU_HEADS, LRU_HEAD_DIM, LRU_HEAD_DIM)), full((1, w)), full((1, w)),
        ],
        out_specs=[full((n, w)), full((n, w)), full((n, (CONV_W - 1) * w))],
        out_shape=[
            jax.ShapeDtypeStruct((n, w), BF16),
            jax.ShapeDtypeStruct((n, w), F32),
            jax.ShapeDtypeStruct((n, (CONV_W - 1) * w), F32),
        ],
        compiler_params=_cparams("arbitrary"),
        name="lru_step",
    )(proj, proj, conv_buf, h0, conv_w, conv_b.reshape(1, w), w_a, b_a.reshape(1, w), w_x,
      b_x.reshape(1, w), lam.reshape(1, w))


def _paged_share(step, pt_ref, q_ref, kn_ref, lat_hbm, kpe_hbm, o_ref,
                 latbuf, kpebuf, sem, latb_ref, s_ref, m_ref, l_ref, acc_ref,
                 *, layer, group, n_groups, sample0, n_samples, step_groups):
    total = n_samples * n_groups
    n_buf = latbuf.shape[0]
    g_base = step * step_groups

    def copies(g, slot):
        gc = jnp.minimum(g, total - 1)
        sb = sample0 + gc // n_groups
        sj = gc % n_groups
        out = []
        for i in range(group):
            page = pt_ref[sb, sj * group + i]
            keys = pl.ds(i * PAGE_SIZE, PAGE_SIZE)
            out.append(pltpu.make_async_copy(lat_hbm.at[layer, page], latbuf.at[slot, keys], sem.at[0, slot]))
            out.append(pltpu.make_async_copy(kpe_hbm.at[layer, page], kpebuf.at[slot, :, keys], sem.at[1, slot]))
        return out

    def wait_group(g):
        for c in copies(g, g % n_buf):
            c.wait()

    def stage_scores(g, parity):
        slot = g % n_buf
        q = q_ref[sample0 + jnp.minimum(g, total - 1) // n_groups]
        lat = latbuf[slot].astype(BF16)
        latb_ref[parity] = lat
        kpe = kpebuf[slot].astype(BF16)
        s = lax.dot_general(q[:, 0:KV_RANK], lat, (((1,), (1,)), ((), ())), preferred_element_type=F32)
        s = s + jnp.dot(q[:, KV_RANK:KV_RANK + QK_ROPE], kpe, preferred_element_type=F32)
        s_ref[parity] = s

    def softmax_step(s):
        m_prev = m_ref[...]
        m_next = jnp.maximum(m_prev, jnp.max(s, axis=1, keepdims=True))
        alpha = jnp.exp(m_prev - m_next)
        p = jnp.exp(s - m_next[:, 0:1])
        l_ref[...] = alpha * l_ref[...] + jnp.sum(p, axis=1, keepdims=True)
        m_ref[...] = m_next
        return p, alpha[:, 0:1]

    @pl.when(step == 0)
    def _():
        for g0 in range(n_buf):
            for c in copies(g0, g0):
                c.start()
        wait_group(0)
        stage_scores(0, 0)

    @pl.when(g_base % n_groups == 0)
    def _():
        m_ref[...] = jnp.full(m_ref.shape, NEG_INF, F32)
        l_ref[...] = jnp.zeros(l_ref.shape, F32)
        acc_ref[...] = jnp.zeros(acc_ref.shape, F32)

    def body(jj, carry):
        for parity in range(2):
            g = g_base + 2 * jj + parity
            wait_group(g + 1)

            @pl.when(g + n_buf <= total)
            def _():
                for c in copies(g + n_buf, g % n_buf):
                    c.start()

            p, alpha = softmax_step(s_ref[parity])
            stage_scores(g + 1, 1 - parity)
            pv = jnp.dot(p.astype(BF16), latb_ref[parity], preferred_element_type=F32)
            acc_ref[...] = acc_ref[...] * alpha + pv
        return carry

    lax.fori_loop(0, step_groups // 2, body, 0)

    @pl.when((g_base + step_groups) % n_groups == 0)
    def _():
        sample = sample0 + g_base // n_groups
        q = q_ref[sample]
        kn = kn_ref[pl.ds(sample, 1), :]
        s_new = jnp.sum(q.astype(F32) * kn, axis=1, keepdims=True)
        p, alpha = softmax_step(s_new)
        acc = acc_ref[...] * alpha + p.astype(BF16).astype(F32) * kn[:, 0:KV_RANK]
        o_ref[0] = acc / l_ref[:, 0:1]


def _ffn_paged_kernel(pt_ref, x_hbm, g_ref, w1_ref, w2_ref, gf_ref, q_ref, kn_ref, lat_hbm, kpe_hbm,
                      o_hbm, att_ref, xbuf, obuf, io_sem, xn_ref, latbuf, kpebuf, sem, latb_ref, s_ref,
                      m_ref, l_ref, acc_ref, *, final_norm, col_chunk, **paged):
    i = pl.program_id(0)
    j = pl.program_id(1)
    ni = pl.num_programs(0)
    nj = pl.num_programs(1)
    tm = xbuf.shape[0]

    def x_copy(tile):
        return pltpu.make_async_copy(x_hbm.at[pl.ds(tile * tm, tm)], xbuf, io_sem.at[0])

    def o_copy(tile):
        return pltpu.make_async_copy(obuf, o_hbm.at[pl.ds(tile * tm, tm)], io_sem.at[1])

    @pl.when((i == 0) & (j == 0))
    def _():
        x_copy(0).start()

    @pl.when(j == 0)
    def _():
        x_copy(i).wait()

    @pl.when((j == 0) & (i > 0))
    def _():
        o_copy(i - 1).wait()

    _ffn_step(xbuf, g_ref, w1_ref, w2_ref, gf_ref, obuf, xn_ref, final_norm=final_norm,
              col_chunk=col_chunk)

    @pl.when((j == 1) & (i + 1 < ni))
    def _():
        x_copy(i + 1).start()

    @pl.when(j == nj - 1)
    def _():
        o_copy(i).start()

    _paged_share(i * nj + j, pt_ref, q_ref, kn_ref, lat_hbm, kpe_hbm, att_ref,
                 latbuf, kpebuf, sem, latb_ref, s_ref, m_ref, l_ref, acc_ref, **paged)

    @pl.when((i == ni - 1) & (j == nj - 1))
    def _():
        o_copy(i).wait()


def _ffn_paged(x, g, w1, w2, g_final, page_table, q, k_new, cache_latent, cache_kpe_t, *,
               mlp_layer, tm, tf, final_norm, cache_layer, group, sample0, n_samples):
    t, d = x.shape
    f = w1.shape[2]
    ni, nj = t // tm, f // tf
    n_pages = page_table.shape[1]
    assert n_pages % group == 0
    n_groups = n_pages // group
    total = n_samples * n_groups
    assert total % (ni * nj) == 0
    step_groups = total // (ni * nj)
    assert step_groups % 2 == 0 and n_groups % step_groups == 0 and total >= FUSED_PAGE_BUFFERS
    assert nj >= 2
    keys = group * PAGE_SIZE
    n_dec = q.shape[0]
    once = pl.Buffered(1)

    def att_index(i, j, pt):
        return ((i * nj + j) * step_groups // n_groups, 0, 0)

    grid_spec = pltpu.PrefetchScalarGridSpec(
        num_scalar_prefetch=1,
        grid=(ni, nj),
        in_specs=[
            pl.BlockSpec(memory_space=pl.ANY),
            pl.BlockSpec((1, d), lambda i, j, pt: (0, 0)),
            pl.BlockSpec((None, d, tf), lambda i, j, pt: (mlp_layer, 0, j)),
            pl.BlockSpec((None, tf, d), lambda i, j, pt: (mlp_layer, j, 0)),
            pl.BlockSpec((1, d), lambda i, j, pt: (0, 0)),
            pl.BlockSpec((n_dec, MLA_HEADS, QK_WIDTH), lambda i, j, pt: (0, 0, 0), pipeline_mode=once),
            pl.BlockSpec((n_dec, QK_WIDTH), lambda i, j, pt: (0, 0), pipeline_mode=once),
            pl.BlockSpec(memory_space=pl.ANY),
            pl.BlockSpec(memory_space=pl.ANY),
        ],
        out_specs=[
            pl.BlockSpec(memory_space=pl.ANY),
            pl.BlockSpec((1, MLA_HEADS, KV_RANK), att_index),
        ],
        scratch_shapes=[
            pltpu.VMEM((tm, d), F32),
            pltpu.VMEM((tm, d), F32),
            pltpu.SemaphoreType.DMA((2,)),
            pltpu.VMEM((tm, d), BF16),
            pltpu.VMEM((FUSED_PAGE_BUFFERS, keys, KV_RANK), F32),
            pltpu.VMEM((FUSED_PAGE_BUFFERS, QK_ROPE, keys), F32),
            pltpu.SemaphoreType.DMA((2, FUSED_PAGE_BUFFERS)),
            pltpu.VMEM((2, keys, KV_RANK), BF16),
            pltpu.VMEM((2, MLA_HEADS, keys), F32),
            pltpu.VMEM((MLA_HEADS, LANES), F32),
            pltpu.VMEM((MLA_HEADS, LANES), F32),
            pltpu.VMEM((MLA_HEADS, KV_RANK), F32),
        ],
    )
    return pl.pallas_call(
        functools.partial(_ffn_paged_kernel, final_norm=final_norm, col_chunk=512, layer=cache_layer,
                          group=group, n_groups=n_groups, sample0=sample0, n_samples=n_samples,
                          step_groups=step_groups),
        grid_spec=grid_spec,
        out_shape=[jax.ShapeDtypeStruct((t, d), F32),
                   jax.ShapeDtypeStruct((n_samples, MLA_HEADS, KV_RANK), F32)],
        compiler_params=_cparams("arbitrary", "arbitrary"),
        name="ffn_paged",
    )(page_table, x, g.reshape(1, d), w1, w2, g_final.reshape(1, d), q, k_new, cache_latent, cache_kpe_t)


def _value_up_kernel(o_ref, wuv_ref, y_ref):
    for h in range(MLA_HEADS):
        oh = o_ref[:, h * KV_RANK:(h + 1) * KV_RANK].astype(BF16)
        y_ref[:, h * V_HEAD:(h + 1) * V_HEAD] = jnp.dot(
            oh, wuv_ref[h], preferred_element_type=F32).astype(BF16)


def _value_up(o_lat, wuv_t):
    n = o_lat.shape[0]
    full = lambda shape: pl.BlockSpec(shape, lambda i: (0,) * len(shape))
    return pl.pallas_call(
        _value_up_kernel,
        grid=(1,),
        in_specs=[full((n, MLA_HEADS * KV_RANK)), full((MLA_HEADS, KV_RANK, V_HEAD))],
        out_specs=full((n, MLA_HEADS * V_HEAD)),
        out_shape=jax.ShapeDtypeStruct((n, MLA_HEADS * V_HEAD), BF16),
        compiler_params=_cparams("arbitrary"),
        name="value_up",
    )(o_lat.reshape(n, MLA_HEADS * KV_RANK), wuv_t)


def _layer_norm(v, g, b):
    mu = jnp.mean(v, axis=-1, keepdims=True)
    var = jnp.mean(jnp.square(v - mu), axis=-1, keepdims=True)
    return (v - mu) * lax.rsqrt(var + EPS) * g + b


def _gate_prompt_kernel(u_ref, v_ref, g_ref, b_ref, ws_ref, bias_ref, o_ref):
    tm = u_ref.shape[0]
    vb = _layer_norm(v_ref[...], g_ref[...], b_ref[...]).astype(BF16)
    row = lax.broadcasted_iota(jnp.int32, (CHUNK, CHUNK), 0)
    col = lax.broadcasted_iota(jnp.int32, (CHUNK, CHUNK), 1)
    causal = col <= row
    for g in range(GMLP_GROUPS):
        wc = jnp.where(causal, ws_ref[g], 0.0).astype(BF16)
        cs = slice(g * GMLP_GROUP_DIM, (g + 1) * GMLP_GROUP_DIM)
        for c in range(tm // CHUNK):
            rs = slice(c * CHUNK, (c + 1) * CHUNK)
            sg = jnp.dot(wc, vb[rs, cs], preferred_element_type=F32) + bias_ref[:, cs]
            o_ref[rs, cs] = (u_ref[rs, cs] * sg).astype(BF16)


def _gate_prompt(z, ln_g, ln_b, w_s, bias, *, tm):
    t = z.shape[0]
    d = GMLP_HALF
    full = lambda shape: pl.BlockSpec(shape, lambda i: (0,) * len(shape))
    return pl.pallas_call(
        _gate_prompt_kernel,
        grid=(t // tm,),
        in_specs=[
            pl.BlockSpec((tm, d), lambda i: (i, 0)),
            pl.BlockSpec((tm, d), lambda i: (i, 1)),
            full((1, d)), full((1, d)), full((GMLP_GROUPS, CHUNK, CHUNK)), full((CHUNK, d)),
        ],
        out_specs=pl.BlockSpec((tm, d), lambda i: (i, 0)),
        out_shape=jax.ShapeDtypeStruct((t, d), BF16),
        compiler_params=_cparams("parallel"),
        name="gate_prompt",
    )(z, z, ln_g.reshape(1, d), ln_b.reshape(1, d), w_s, bias)


def _gate_step_kernel(u_ref, v_ref, g_ref, b_ref, w0_ref, b0_ref, o_ref, vout_ref):
    v = _layer_norm(v_ref[...], g_ref[...], b_ref[...])
    vout_ref[...] = v
    sg = w0_ref[...].astype(BF16).astype(F32) * v.astype(BF16).astype(F32) + b0_ref[...]
    o_ref[...] = (u_ref[...] * sg).astype(BF16)


def _gate_step(z, ln_g, ln_b, w0, b0):
    n = z.shape[0]
    d = GMLP_HALF
    full = lambda shape: pl.BlockSpec(shape, lambda i: (0,) * len(shape))
    return pl.pallas_call(
        _gate_step_kernel,
        grid=(1,),
        in_specs=[
            pl.BlockSpec((n, d), lambda i: (0, 0)),
            pl.BlockSpec((n, d), lambda i: (0, 1)),
            full((1, d)), full((1, d)), full((1, d)), full((1, d)),
        ],
        out_specs=[full((n, d)), full((n, d))],
        out_shape=[jax.ShapeDtypeStruct((n, d), BF16), jax.ShapeDtypeStruct((n, d), F32)],
        compiler_params=_cparams("arbitrary"),
        name="gate_step",
    )(z, z, ln_g.reshape(1, d), ln_b.reshape(1, d), w0.reshape(1, d), b0.reshape(1, d))


def _matmul_res_kernel(a_ref, w_ref, r_ref, o_ref):
    o_ref[...] = r_ref[...] + jnp.dot(a_ref[...], w_ref[...], preferred_element_type=F32)


def _matmul_res(a, w, res, *, tm, tn):
    t, k = a.shape
    n = w.shape[1]
    return pl.pallas_call(
        _matmul_res_kernel,
        grid=(t // tm, n // tn),
        in_specs=[
            pl.BlockSpec((tm, k), lambda i, j: (i, 0)),
            pl.BlockSpec((k, tn), lambda i, j: (0, j)),
            pl.BlockSpec((tm, tn), lambda i, j: (i, j)),
        ],
        out_specs=pl.BlockSpec((tm, tn), lambda i, j: (i, j)),
        out_shape=jax.ShapeDtypeStruct((t, n), F32),
        compiler_params=_cparams("parallel", "arbitrary"),
        name="matmul_res",
    )(a, w, res)


def _pack_w_in(w_in):
    q_dim = MLA_HEADS * (QK_NOPE + QK_ROPE)
    q = w_in[:, :q_dim].reshape(D_MODEL, MLA_HEADS, QK_NOPE + QK_ROPE)
    qn = q[:, :, :QK_NOPE].reshape(D_MODEL, MLA_HEADS * QK_NOPE)
    qp = q[:, :, QK_NOPE:].reshape(D_MODEL, MLA_HEADS * QK_ROPE)
    ckv = w_in[:, q_dim:q_dim + KV_RANK]
    kpe = w_in[:, q_dim + KV_RANK:q_dim + KV_RANK + QK_ROPE]
    xb = w_in[:, q_dim + KV_RANK + QK_ROPE:q_dim + KV_RANK + QK_ROPE + LRU_WIDTH]
    gb = w_in[:, q_dim + KV_RANK + QK_ROPE + LRU_WIDTH:]
    pad = jnp.zeros((D_MODEL, LANES - QK_ROPE), w_in.dtype)
    packed = jnp.concatenate([qn, xb, gb, qp, ckv, kpe, pad], axis=1)
    assert packed.shape[1] == PROJ_WIDTH
    return packed.astype(BF16)


def _rope_tables(pos):
    half = QK_ROPE // 2
    inv = ROPE_THETA ** (-jnp.arange(half, dtype=F32) / half)
    ang = pos.astype(F32)[:, None] * inv[None, :]
    cos = jnp.cos(ang)
    sin = jnp.sin(ang)
    reps = LANES // QK_ROPE
    cos_tab = jnp.tile(jnp.concatenate([cos, cos], axis=1), (1, reps))
    sin_tab = jnp.tile(jnp.concatenate([-sin, sin], axis=1), (1, reps))
    return cos_tab, sin_tab


class _Plan(NamedTuple):
    tm_prompt: int
    tm_sample: int
    t_small: int
    tq: int
    tk: int
    page_group: int
    tf: int
    tf_sample: int


def _plan(batch, seq, n_dec, n_pages):
    t = batch * seq
    tm_prompt = 1024 if t % 1024 == 0 else t
    t_small = 512 if seq % 512 == 0 else seq
    tk = 512 if seq % 512 == 0 else seq
    tq = 1024 if seq % 1024 == 0 else tk
    return _Plan(tm_prompt=tm_prompt, tm_sample=n_dec, t_small=t_small, tq=tq, tk=tk,
                 page_group=16 if n_pages % 16 == 0 else 2, tf=512, tf_sample=1024)


def kernel(x_prompt, x_sample, cache_latent, cache_kpe, state_lru_h, state_conv, page_table,
           norm_mix, norm_ffn, norm_final, ab_w_in, ab_g_kv, ab_w_uk, ab_w_uv, ab_conv_w, ab_conv_b,
           ab_w_a, ab_b_a, ab_w_x, ab_b_x, ab_lambda, ab_w_out, c_w_in, c_ln_g, c_ln_b, c_w_s, c_b_s,
           c_w_out, ffn_w1, ffn_w2):
    batch, seq, _ = x_prompt.shape
    n_dec, dec_seq, _ = x_sample.shape
    assert dec_seq == 1
    n_pages = page_table.shape[1]
    past_len = n_pages * PAGE_SIZE
    depth = norm_mix.shape[0]

    cos_p, sin_p = _rope_tables(jnp.arange(seq, dtype=jnp.int32))
    cos_s, sin_s = _rope_tables(jnp.full((n_dec,), past_len, jnp.int32))
    cache_kpe_t = jnp.swapaxes(cache_kpe, 2, 3)
    w1_all = ffn_w1.astype(BF16)
    w2_all = ffn_w2.astype(BF16)

    hp = x_prompt.reshape(batch * seq, D_MODEL)
    hs = x_sample.reshape(n_dec, D_MODEL)
    plan = _plan(batch, seq, n_dec, n_pages)
    tp = plan.tm_prompt
    tsm = plan.tm_sample

    assert depth == 2
    mix_w = _pack_w_in(ab_w_in[0])
    wuk_t = jnp.transpose(ab_w_uk[0], (1, 2, 0)).astype(BF16)
    wuv_t = jnp.transpose(ab_w_uv[0], (1, 0, 2)).astype(BF16)
    mix_out = ab_w_out[0].astype(BF16)
    lru_args = (ab_conv_w[0], ab_conv_b[0], ab_w_a[0].astype(BF16), ab_b_a[0], ab_w_x[0].astype(BF16),
                ab_b_x[0], ab_lambda[0])
    gm_w_in = c_w_in[0].astype(BF16)
    gm_w_out = c_w_out[0].astype(BF16)
    gm_bias = jnp.repeat(c_b_s[0].T, GMLP_GROUP_DIM, axis=1)
    gm_w0 = jnp.repeat(c_w_s[0][:, 0, 0], GMLP_GROUP_DIM)

    proj_s = _norm_matmul(hs, norm_mix[0], mix_w, tm=tsm, tn=PROJ_WIDTH // 3)
    q_s, k_s, lat_s, kpe_s = _prep(proj_s, cos_s, sin_s, ab_g_kv[0], wuk_t, tm=tsm)
    q_s = jnp.transpose(q_s, (1, 0, 2))
    kn_s = k_s.astype(F32)
    half = n_dec // 2
    paged = dict(tm=tp, tf=plan.tf, cache_layer=0, group=plan.page_group)

    proj = _norm_matmul(hp, norm_mix[0], mix_w, tm=tp, tn=PROJ_WIDTH // 3)
    wuk_n = ab_w_uk[0].reshape(KV_RANK, MLA_HEADS * QK_NOPE).astype(BF16)
    wuv_n = ab_w_uv[0].reshape(KV_RANK, MLA_HEADS * V_HEAD).astype(BF16)
    q, k, v, lat_p, kpe_p = _prep_prompt(proj, cos_p, sin_p, ab_g_kv[0], wuk_n, wuv_n, tm=plan.t_small)
    o_mla = _flash_prompt(q, k, v, batch=batch, seq=seq, tq=plan.tq, tk=plan.tk)
    o_lru, lru_p, conv_p = _lru_prompt(proj, *lru_args, batch=batch, seq=seq, ts=plan.t_small)
    hp = _pair_matmul_res(o_mla, o_lru, mix_out, hp, tm=tp, tn=1024)
    hp, att_a = _ffn_paged(hp, norm_ffn[0], w1_all, w2_all, norm_final, page_table, q_s, kn_s,
                           cache_latent, cache_kpe_t, mlp_layer=0, final_norm=False,
                           sample0=0, n_samples=half, **paged)

    z = _norm_matmul(hp, norm_mix[1], gm_w_in, tm=tp, tn=1024, act="gelu")
    a = _gate_prompt(z, c_ln_g[0], c_ln_b[0], c_w_s[0], gm_bias, tm=plan.t_small)
    hp = _matmul_res(a, gm_w_out, hp, tm=tp, tn=1024)
    hp, att_b = _ffn_paged(hp, norm_ffn[1], w1_all, w2_all, norm_final, page_table, q_s, kn_s,
                           cache_latent, cache_kpe_t, mlp_layer=1, final_norm=True,
                           sample0=half, n_samples=n_dec - half, **paged)

    o_mla = _value_up(jnp.concatenate([att_a, att_b], axis=0), wuv_t)
    o_lru, lru_s, conv_s = _lru_step(
        proj_s, state_conv[0].reshape(n_dec, (CONV_W - 1) * LRU_WIDTH), state_lru_h[0], *lru_args)
    hs = _pair_matmul_res(o_mla, o_lru, mix_out, hs, tm=tsm, tn=1024)
    hs = _ffn(hs, norm_ffn[0], w1_all, w2_all, norm_final, layer=0, tm=tsm, tf=plan.tf_sample,
              final_norm=False)

    z = _norm_matmul(hs, norm_mix[1], gm_w_in, tm=tsm, tn=1024, act="gelu")
    a, v_s = _gate_step(z, c_ln_g[0], c_ln_b[0], gm_w0, gm_bias[0])
    hs = _matmul_res(a, gm_w_out, hs, tm=tsm, tn=1024)
    hs = _ffn(hs, norm_ffn[1], w1_all, w2_all, norm_final, layer=1, tm=tsm, tf=plan.tf_sample,
              final_norm=True)

    return (hp.reshape(batch, seq, D_MODEL), hs.reshape(n_dec, 1, D_MODEL),
            lat_p.reshape(1, batch, seq, KV_RANK), kpe_p.reshape(1, batch, seq, QK_ROPE),
            lru_p.reshape(1, batch, LRU_WIDTH), conv_p[None],
            lat_s.reshape(1, n_dec, 1, KV_RANK), kpe_s.reshape(1, n_dec, 1, QK_ROPE),
            lru_s[None], conv_s.reshape(1, n_dec, CONV_W - 1, LRU_WIDTH),
            v_s.reshape(1, n_dec, 1, GMLP_HALF))
```
